```python
import math
import jax, jax.numpy as jnp
from jax import lax
import numpy as np

D_MODEL = 1024
BATCH = 1
SEQ = 16384
DEPTH = 2

GRID_W = 64
CTX_LEN = 256
RMS_EPS = 1e-6
N_MOD = 6

SCONV_W = 3
CONV_CH = D_MODEL // 2
SSM_CH = D_MODEL // 2
SSM_H = 16
SSM_G = SSM_CH // SSM_H
SSM_P = 64
EVEN_IN = 3 * CONV_CH + SSM_CH
EVEN_MIX = CONV_CH + SSM_CH

DN_HEADS = 8
DN_DK = 128
DN_DV = 128
DN_QK = DN_HEADS * DN_DK
DN_V = DN_HEADS * DN_DV
DN_CONV_CH = 2 * DN_QK + DN_V
ODD_IN = 2 * DN_QK + 2 * DN_V + 4 * DN_HEADS
DN_CHUNK = 64

PEER_HEADS = 8
PEER_KEYS = 128
PEER_EXPERTS = PEER_KEYS * PEER_KEYS
PEER_TOPK = 16
PEER_DKEY = 256
PEER_BLOCK = 128

N_EVEN = (DEPTH + 1) // 2
N_ODD = DEPTH // 2

kernel_name = 'hybrid_conv_s5_gdn_peer_dit'


def rmsnorm(x, g):
    xf = x.astype(jnp.float32)
    y = xf * lax.rsqrt(jnp.mean(xf * xf, axis=-1, keepdims=True) + RMS_EPS)
    return (y * g.astype(jnp.float32)).astype(x.dtype)


def modulate(h, shift, scale):
    return h * (1 + scale) + shift


def l2norm(x):
    return x * lax.rsqrt(jnp.sum(x * x, axis=-1, keepdims=True) + 1e-6)


def dwconv3(x, w):
    L = x.shape[-2]
    pad = [(0, 0)] * (x.ndim - 2) + [(1, 1), (0, 0)]
    xp = jnp.pad(x, pad)
    return xp[..., 0:L, :] * w[0] + xp[..., 1:L + 1, :] * w[1] + xp[..., 2:L + 2, :] * w[2]


def grid_conv(x, w):
    B, L, C = x.shape
    rows = L // GRID_W
    return dwconv3(x.reshape(B, rows, GRID_W, C), w).reshape(B, L, C)


def s5_discretize(lam_re, lam_im, log_dt, b_re, b_im):
    lam_re = jnp.minimum(lam_re.astype(jnp.float32), -1e-4)
    lam_im = lam_im.astype(jnp.float32)
    dt = jnp.exp(log_dt.astype(jnp.float32))[:, None]
    mag = jnp.exp(lam_re * dt)
    ab_re = mag * jnp.cos(lam_im * dt)
    ab_im = mag * jnp.sin(lam_im * dt)
    den = lam_re * lam_re + lam_im * lam_im
    f_re = ((ab_re - 1) * lam_re + ab_im * lam_im) / den
    f_im = (ab_im * lam_re - (ab_re - 1) * lam_im) / den
    b_re = b_re.astype(jnp.float32)
    b_im = b_im.astype(jnp.float32)
    bb_re = f_re[..., None] * b_re - f_im[..., None] * b_im
    bb_im = f_re[..., None] * b_im + f_im[..., None] * b_re
    return ab_re, ab_im, bb_re, bb_im


def complex_affine_combine(e1, e2):
    a1r, a1i, b1r, b1i = e1
    a2r, a2i, b2r, b2i = e2
    ar = a1r * a2r - a1i * a2i
    ai = a1r * a2i + a1i * a2r
    br = a2r * b1r - a2i * b1i + b2r
    bi = a2r * b1i + a2i * b1r + b2i
    return ar, ai, br, bi


def s5_scan(ab_re, ab_im, bu_re, bu_im, reverse, s0=None):
    a_re = jnp.broadcast_to(ab_re, bu_re.shape)
    a_im = jnp.broadcast_to(ab_im, bu_im.shape)
    A_re, A_im, s_re, s_im = lax.associative_scan(
        complex_affine_combine, (a_re, a_im, bu_re, bu_im), reverse=reverse, axis=1)
    if s0 is not None:
        r0 = s0[0][:, None]
        i0 = s0[1][:, None]
        s_re, s_im = s_re + A_re * r0 - A_im * i0, s_im + A_re * i0 + A_im * r0
    return s_re, s_im


def s5_bidir(u_ctx, u_lat, lam_re, lam_im, log_dt, b_re, b_im, c_re, c_im, d_skip, need_ctx_out):
    B, Lc, _ = u_ctx.shape
    L = u_lat.shape[1]
    uc = u_ctx.astype(jnp.float32).reshape(B, Lc, SSM_G, SSM_H)
    ul = u_lat.astype(jnp.float32).reshape(B, L, SSM_G, SSM_H)
    dsk = d_skip.astype(jnp.float32).reshape(SSM_G, SSM_H)
    y_lat = ul * dsk
    y_ctx = uc * dsk
    for di in range(2):
        rev = di == 1
        ab_re, ab_im, bb_re, bb_im = s5_discretize(lam_re[di], lam_im[di], log_dt[di], b_re[di], b_im[di])
        cr = c_re[di].astype(jnp.float32)
        ci = c_im[di].astype(jnp.float32)
        bu_c = (jnp.einsum('blgh,gph->blgp', uc, bb_re), jnp.einsum('blgh,gph->blgp', uc, bb_im))
        bu_l = (jnp.einsum('blgh,gph->blgp', ul, bb_re), jnp.einsum('blgh,gph->blgp', ul, bb_im))
        sc_re, sc_im = s5_scan(ab_re, ab_im, bu_c[0], bu_c[1], rev)
        end = 0 if rev else -1
        sl_re, sl_im = s5_scan(ab_re, ab_im, bu_l[0], bu_l[1], rev, (sc_re[:, end], sc_im[:, end]))
        y_lat = y_lat + jnp.einsum('blgp,ghp->blgh', sl_re, cr) - jnp.einsum('blgp,ghp->blgh', sl_im, ci)
        if need_ctx_out:
            y_ctx = y_ctx + jnp.einsum('blgp,ghp->blgh', sc_re, cr) - jnp.einsum('blgp,ghp->blgh', sc_im, ci)
    y_lat = y_lat.reshape(B, L, SSM_CH).astype(u_lat.dtype)
    y_ctx = y_ctx.reshape(B, Lc, SSM_CH).astype(u_ctx.dtype) if need_ctx_out else None
    return y_ctx, y_lat


def even_mixer(a_ctx, a_lat, w_in, conv_w, lam_re, lam_im, log_dt, b_re, b_im, c_re, c_im,
               d_skip, glu_w, glu_b, w_out, need_ctx_out):
    p_ctx = a_ctx @ w_in
    p_lat = a_lat @ w_in

    def conv_branch(p, conv_fn):
        u = p[..., :CONV_CH]
        g_in = p[..., CONV_CH:2 * CONV_CH]
        g_out = p[..., 2 * CONV_CH:3 * CONV_CH]
        return g_out * conv_fn(g_in * u, conv_w)

    def half_glu(y):
        y = jax.nn.gelu(y)
        return y * jax.nn.sigmoid(y @ glu_w + glu_b)

    s_ctx, s_lat = s5_bidir(p_ctx[..., 3 * CONV_CH:], p_lat[..., 3 * CONV_CH:], lam_re, lam_im, log_dt,
                            b_re, b_im, c_re, c_im, d_skip, need_ctx_out)
    y_lat = jnp.concatenate([conv_branch(p_lat, grid_conv), half_glu(s_lat)], axis=-1) @ w_out
    y_ctx = None
    if need_ctx_out:
        y_ctx = jnp.concatenate([conv_branch(p_ctx, dwconv3), half_glu(s_ctx)], axis=-1) @ w_out
    return y_ctx, y_lat


def gdn_chunk(q, k, v, g, beta, s0):
    B, H, L, dk = q.shape
    dv = v.shape[-1]
    C = DN_CHUNK
    N = L // C
    q = q.reshape(B, H, N, C, dk)
    k = k.reshape(B, H, N, C, dk)
    v = v.reshape(B, H, N, C, dv)
    g = jnp.cumsum(g.reshape(B, H, N, C), axis=-1)
    beta = beta.reshape(B, H, N, C)
    kb = k * beta[..., None]
    vb = v * beta[..., None]
    incl = jnp.tril(jnp.ones((C, C), dtype=bool))
    strict = jnp.tril(jnp.ones((C, C), dtype=bool), -1)
    diff = g[..., :, None] - g[..., None, :]
    decay = jnp.where(incl, jnp.exp(jnp.where(incl, diff, 0.0)), 0.0)
    a = jnp.where(strict, jnp.einsum('bhnik,bhnjk->bhnij', kb, k) * decay, 0.0)
    eye = jnp.eye(C, dtype=q.dtype)
    tinv = lax.linalg.triangular_solve(eye + a, jnp.broadcast_to(eye, a.shape), left_side=True,
                                       lower=True, unit_diagonal=True)
    u = jnp.einsum('bhnij,bhnjv->bhniv', tinv, vb)
    w = jnp.einsum('bhnij,bhnjk->bhnik', tinv, kb * jnp.exp(g)[..., None])
    attn = jnp.einsum('bhnik,bhnjk->bhnij', q, k) * decay
    g_last = g[..., -1]
    q_dec = q * jnp.exp(g)[..., None]
    k_dec = k * jnp.exp(g_last[..., None] - g)[..., None]

    def step(S, inp):
        qd, kd, ui, wi, ai, gl = inp
        v_new = ui - jnp.einsum('bhck,bhkv->bhcv', wi, S)
        o = jnp.einsum('bhck,bhkv->bhcv', qd, S) + jnp.einsum('bhij,bhjv->bhiv', ai, v_new)
        S = S * jnp.exp(gl)[..., None, None] + jnp.einsum('bhck,bhcv->bhkv', kd, v_new)
        return S, o

    xs = tuple(jnp.moveaxis(t_, 2, 0) for t_ in (q_dec, k_dec, u, w, attn, g_last))
    s_final, o = lax.scan(step, s0, xs)
    return jnp.moveaxis(o, 0, 2).reshape(B, H, L, dv), s_final


def gdn_prep(p, conv_fn, conv_w, a_log, dt_bias):
    B, L, _ = p.shape
    qkv = jax.nn.silu(conv_fn(p[..., :DN_CONV_CH], conv_w)).astype(jnp.float32)
    z = p[..., DN_CONV_CH:DN_CONV_CH + DN_V]
    o0 = DN_CONV_CH + DN_V
    a = p[..., o0:o0 + 2 * DN_HEADS].astype(jnp.float32).reshape(B, L, 2, DN_HEADS)
    b = p[..., o0 + 2 * DN_HEADS:].astype(jnp.float32).reshape(B, L, 2, DN_HEADS)
    heads = lambda t, dh: t.reshape(B, L, DN_HEADS, dh).transpose(0, 2, 1, 3)
    q = l2norm(heads(qkv[..., :DN_QK], DN_DK)) * (DN_DK ** -0.5)
    k = l2norm(heads(qkv[..., DN_QK:2 * DN_QK], DN_DK))
    v = heads(qkv[..., 2 * DN_QK:], DN_DV)
    g = -jnp.exp(a_log.astype(jnp.float32)) * jax.nn.softplus(a + dt_bias.astype(jnp.float32))
    beta = jax.nn.sigmoid(b)
    return q, k, v, g.transpose(2, 0, 3, 1), beta.transpose(2, 0, 3, 1), z


def gdn_mixer(a_ctx, a_lat, w_in, conv_w, a_log, dt_bias, norm_g, w_out, need_ctx_out):
    qc, kc, vc, gc, bc, zc = gdn_prep(a_ctx @ w_in, dwconv3, conv_w, a_log, dt_bias)
    ql, kl, vl, gl, bl, zl = gdn_prep(a_lat @ w_in, grid_conv, conv_w, a_log, dt_bias)
    B = a_lat.shape[0]
    zero = jnp.zeros((B, DN_HEADS, DN_DK, DN_DV), jnp.float32)
    flip = lambda t: jnp.flip(t, axis=2)
    oc_f, sc_f = gdn_chunk(qc, kc, vc, gc[0], bc[0], zero)
    oc_b, sc_b = gdn_chunk(flip(qc), flip(kc), flip(vc), flip(gc[1]), flip(bc[1]), zero)
    ol_f, _ = gdn_chunk(ql, kl, vl, gl[0], bl[0], sc_f)
    ol_b, _ = gdn_chunk(flip(ql), flip(kl), flip(vl), flip(gl[1]), flip(bl[1]), sc_b)

    def readout(o, z):
        Bz, L, _ = z.shape
        o = o.transpose(0, 2, 1, 3)
        y = rmsnorm(o, norm_g) * jax.nn.silu(z.astype(jnp.float32)).reshape(Bz, L, DN_HEADS, DN_DV)
        return y.reshape(Bz, L, DN_V).astype(z.dtype) @ w_out

    y_lat = readout(ol_f + flip(ol_b), zl)
    y_ctx = readout(oc_f + flip(oc_b), zc) if need_ctx_out else None
    return y_ctx, y_lat


def peer(h, wq, subkeys, u_tab, v_tab):
    B, L, D = h.shape
    T = B * L
    hf = h.reshape(T, D)
    q = (hf @ wq).astype(jnp.float32).reshape(T, PEER_HEADS, 2, PEER_DKEY // 2)
    s = jnp.einsum('thsk,snk->thsn', q, subkeys.astype(jnp.float32))
    s_top, i_top = lax.top_k(s, PEER_TOPK)
    kk = PEER_TOPK * PEER_TOPK
    cand = (s_top[:, :, 0, :, None] + s_top[:, :, 1, None, :]).reshape(T, PEER_HEADS, kk)
    cand_idx = (i_top[:, :, 0, :, None] * PEER_KEYS + i_top[:, :, 1, None, :]).reshape(T, PEER_HEADS, kk)
    best, pos = lax.top_k(cand, PEER_TOPK)
    idx = jnp.take_along_axis(cand_idx, pos, axis=-1)
    gate = jax.nn.softmax(best, axis=-1).astype(h.dtype)
    nb = T // PEER_BLOCK

    def block(args):
        xb, ib, gb = args
        act = jax.nn.gelu(jnp.einsum('td,thkd->thk', xb, u_tab[ib]))
        return jnp.einsum('thk,thkd->td', act * gb, v_tab[ib])

    y = lax.map(block, (hf.reshape(nb, PEER_BLOCK, D),
                        idx.reshape(nb, PEER_BLOCK, PEER_HEADS, PEER_TOPK),
                        gate.reshape(nb, PEER_BLOCK, PEER_HEADS, PEER_TOPK)))
    return y.reshape(B, L, D)


def setup_inputs(seed: int = 0) -> dict:
    key = jax.random.key(seed)
    ks = iter(jax.random.split(key, 48))
    f32 = jnp.float32
    nrm = lambda shape, s: jax.random.normal(next(ks), shape, f32) * s
    D = D_MODEL
    x = nrm((BATCH, SEQ, D), 1.0)
    c = nrm((BATCH, D), 1.0)
    ctx = nrm((BATCH, CTX_LEN, D), 1.0)
    c_ctx = nrm((D,), 1.0)
    ada_w = nrm((DEPTH, D, N_MOD * D), 0.5 * D ** -0.5)
    ada_b = nrm((DEPTH, N_MOD * D), 0.01)
    norm1_g = 1.0 + nrm((DEPTH, D), 0.05)
    norm2_g = 1.0 + nrm((DEPTH, D), 0.05)
    e_w_in = nrm((N_EVEN, D, EVEN_IN), D ** -0.5)
    e_conv_w = nrm((N_EVEN, SCONV_W, CONV_CH), 0.5)
    n_idx = jnp.arange(SSM_P, dtype=f32)
    s5_lam_re = -0.5 + nrm((N_EVEN, 2, SSM_G, SSM_P), 0.01)
    s5_lam_im = math.pi * n_idx + nrm((N_EVEN, 2, SSM_G, SSM_P), 0.01)
    s5_log_dt = jax.random.uniform(next(ks), (N_EVEN, 2, SSM_G), f32, math.log(1e-3), math.log(1e-1))
    s5_b_re = nrm((N_EVEN, 2, SSM_G, SSM_P, SSM_H), (2 * SSM_H) ** -0.5)
    s5_b_im = nrm((N_EVEN, 2, SSM_G, SSM_P, SSM_H), (2 * SSM_H) ** -0.5)
    s5_c_re = nrm((N_EVEN, 2, SSM_G, SSM_H, SSM_P), (2 * SSM_P) ** -0.5)
    s5_c_im = nrm((N_EVEN, 2, SSM_G, SSM_H, SSM_P), (2 * SSM_P) ** -0.5)
    s5_d = nrm((N_EVEN, SSM_CH), 1.0)
    s5_glu_w = nrm((N_EVEN, SSM_CH, SSM_CH), SSM_CH ** -0.5)
    s5_glu_b = nrm((N_EVEN, SSM_CH), 0.01)
    e_w_out = nrm((N_EVEN, EVEN_MIX, D), EVEN_MIX ** -0.5)
    o_w_in = nrm((N_ODD, D, ODD_IN), D ** -0.5)
    dn_conv_w = nrm((N_ODD, SCONV_W, DN_CONV_CH), 0.5)
    dn_a_log = jnp.log(jax.random.uniform(next(ks), (N_ODD, 2, DN_HEADS), f32, 1.0, 16.0))
    dt0 = jnp.exp(jax.random.uniform(next(ks), (N_ODD, 2, DN_HEADS), f32, math.log(1e-3), math.log(1e-1)))
    dn_dt_bias = dt0 + jnp.log(-jnp.expm1(-dt0))
    dn_norm_g = 1.0 + nrm((N_ODD, DN_DV), 0.05)
    o_w_out = nrm((N_ODD, DN_V, D), DN_V ** -0.5)
    peer_wq = nrm((DEPTH, D, PEER_HEADS * PEER_DKEY), D ** -0.5)
    peer_subkeys = nrm((DEPTH, 2, PEER_KEYS, PEER_DKEY // 2), (PEER_DKEY // 2) ** -0.5)
    peer_u = nrm((DEPTH, PEER_EXPERTS, D), D ** -0.5)
    peer_v = nrm((DEPTH, PEER_EXPERTS, D), 0.5)
    final_g = 1.0 + nrm((D,), 0.05)
    return {'x': x, 'c': c, 'ctx': ctx, 'c_ctx': c_ctx, 'ada_w': ada_w, 'ada_b': ada_b,
            'norm1_g': norm1_g, 'norm2_g': norm2_g, 'e_w_in': e_w_in, 'e_conv_w': e_conv_w,
            's5_lam_re': s5_lam_re, 's5_lam_im': s5_lam_im, 's5_log_dt': s5_log_dt,
            's5_b_re': s5_b_re, 's5_b_im': s5_b_im, 's5_c_re': s5_c_re, 's5_c_im': s5_c_im,
            's5_d': s5_d, 's5_glu_w': s5_glu_w, 's5_glu_b': s5_glu_b, 'e_w_out': e_w_out,
            'o_w_in': o_w_in, 'dn_conv_w': dn_conv_w, 'dn_a_log': dn_a_log, 'dn_dt_bias': dn_dt_bias,
            'dn_norm_g': dn_norm_g, 'o_w_out': o_w_out, 'peer_wq': peer_wq, 'peer_subkeys': peer_subkeys,
            'peer_u': peer_u, 'peer_v': peer_v, 'final_g': final_g}


def reference(x, c, ctx, c_ctx, ada_w, ada_b, norm1_g, norm2_g, e_w_in, e_conv_w,
              s5_lam_re, s5_lam_im, s5_log_dt, s5_b_re, s5_b_im, s5_c_re, s5_c_im,
              s5_d, s5_glu_w, s5_glu_b, e_w_out, o_w_in, dn_conv_w, dn_a_log, dn_dt_bias,
              dn_norm_g, o_w_out, peer_wq, peer_subkeys, peer_u, peer_v, final_g):
    h_lat = x
    h_ctx = ctx
    sc_lat = jax.nn.silu(c)
    sc_ctx = jax.nn.silu(c_ctx)
    for l in range(DEPTH):
        last = l == DEPTH - 1
        i = l // 2
        m_lat = jnp.split((sc_lat @ ada_w[l] + ada_b[l])[:, None, :], N_MOD, axis=-1)
        m_ctx = jnp.split(sc_ctx @ ada_w[l] + ada_b[l], N_MOD, axis=-1)
        a_lat = modulate(rmsnorm(h_lat, norm1_g[l]), m_lat[0], m_lat[1])
        a_ctx = modulate(rmsnorm(h_ctx, norm1_g[l]), m_ctx[0], m_ctx[1])
        if l % 2 == 0:
            y_ctx, y_lat = even_mixer(a_ctx, a_lat, e_w_in[i], e_conv_w[i], s5_lam_re[i], s5_lam_im[i],
                                      s5_log_dt[i], s5_b_re[i], s5_b_im[i], s5_c_re[i], s5_c_im[i],
                                      s5_d[i], s5_glu_w[i], s5_glu_b[i], e_w_out[i], not last)
        else:
            y_ctx, y_lat = gdn_mixer(a_ctx, a_lat, o_w_in[i], dn_conv_w[i], dn_a_log[i], dn_dt_bias[i],
                                     dn_norm_g[i], o_w_out[i], not last)
        h_lat = h_lat + m_lat[2] * y_lat
        h_lat = h_lat + m_lat[5] * peer(modulate(rmsnorm(h_lat, norm2_g[l]), m_lat[3], m_lat[4]),
                                        peer_wq[l], peer_subkeys[l], peer_u[l], peer_v[l])
        if not last:
            h_ctx = h_ctx + m_ctx[2] * y_ctx
            h_ctx = h_ctx + m_ctx[5] * peer(modulate(rmsnorm(h_ctx, norm2_g[l]), m_ctx[3], m_ctx[4]),
                                            peer_wq[l], peer_subkeys[l], peer_u[l], peer_v[l])
    return rmsnorm(h_lat, final_g)
```

```python
import functools
import math

import jax
import jax.numpy as jnp
from jax import lax
from jax.experimental import pallas as pl
from jax.experimental.pallas import tpu as pltpu

F32 = jnp.float32
BF16 = jnp.bfloat16
HI = lax.Precision.HIGHEST

D_MODEL = 1024
GRID_W = 64
RMS_EPS = 1e-6
N_MOD = 6
CONV_CH = 512
SSM_CH = 512
SSM_H = 16
SSM_G = 32
SSM_P = 64
SSM_STATE = SSM_G * SSM_P
DN_HEADS = 8
DN_DK = 128
DN_CHUNK = 64
DN_CONV_CH = 3072
DN_IN_PAD = 4224
PEER_HEADS = 8
PEER_KEYS = 128
PEER_TOPK = 16
PEER_EXPERTS = PEER_KEYS * PEER_KEYS

SUBLANES = 8
LANES = 128
VMEM_LIMIT = 52 * 1024 * 1024

NT_DIMS = (((1,), (1,)), ((), ()))
TN_DIMS = (((0,), (0,)), ((), ()))


def _cparams(*sem):
    return pltpu.CompilerParams(dimension_semantics=sem, vmem_limit_bytes=VMEM_LIMIT)


def _silu(x):
    return x * jax.nn.sigmoid(x)


def _adaln_kernel(c_ref, w_ref, b_ref, o_ref):
    sc = _silu(c_ref[...])
    o_ref[0] = jnp.dot(sc, w_ref[0], precision=HI, preferred_element_type=F32) + b_ref[0]


def adaln(cv, ada_w, ada_b):
    L, D, N = ada_w.shape
    tn = 1536
    return pl.pallas_call(
        _adaln_kernel,
        grid=(L, N // tn),
        in_specs=[pl.BlockSpec((SUBLANES, D), lambda l, j: (0, 0)),
                  pl.BlockSpec((1, D, tn), lambda l, j: (l, 0, j)),
                  pl.BlockSpec((1, 1, tn), lambda l, j: (l, 0, j))],
        out_specs=pl.BlockSpec((1, SUBLANES, tn), lambda l, j: (l, 0, j)),
        out_shape=jax.ShapeDtypeStruct((L, SUBLANES, N), F32),
        compiler_params=_cparams("parallel", "parallel"),
        name="adaln",
    )(cv, ada_w, ada_b.reshape(L, 1, N))


def _nmm_kernel(x_ref, g_ref, sh_ref, sc_ref, w_ref, o_ref, a_ref):
    @pl.when(pl.program_id(1) == 0)
    def _():
        x = x_ref[...]
        y = x * lax.rsqrt(jnp.mean(x * x, axis=-1, keepdims=True) + RMS_EPS)
        a = (y * g_ref[...]) * (1.0 + sc_ref[...]) + sh_ref[...]
        a_ref[...] = a.astype(BF16)

    o_ref[...] = jnp.dot(a_ref[...], w_ref[...], preferred_element_type=F32)


def nmm(x, g, shift, scale, w_bf, tn):
    T, D = x.shape
    N = w_bf.shape[1]
    tm = min(T, 512)
    row = lambda v: v.reshape(1, D)
    return pl.pallas_call(
        _nmm_kernel,
        grid=(T // tm, N // tn),
        in_specs=[pl.BlockSpec((tm, D), lambda i, j: (i, 0)),
                  pl.BlockSpec((1, D), lambda i, j: (0, 0)),
                  pl.BlockSpec((1, D), lambda i, j: (0, 0)),
                  pl.BlockSpec((1, D), lambda i, j: (0, 0)),
                  pl.BlockSpec((D, tn), lambda i, j: (0, j))],
        out_specs=[pl.BlockSpec((tm, tn), lambda i, j: (i, j)),
                   pl.BlockSpec((tm, D), lambda i, j: (i, 0))],
        out_shape=[jax.ShapeDtypeStruct((T, N), F32),
                   jax.ShapeDtypeStruct((T, D), BF16)],
        compiler_params=_cparams("parallel", "arbitrary"),
        name="nmm",
    )(x, row(g), row(shift), row(scale), w_bf)


S5_COLS = 512


def _s5_kernel(u_ref, bd_ref, cd_ref, a_ref, init_ref, y_ref, ends_ref, s_scr, st_scr,
               *, reverse, tj, emit_y):
    @pl.when(pl.program_id(0) == 0)
    def _():
        st_scr[...] = init_ref[...]

    s_scr[...] = jnp.dot(u_ref[...].astype(BF16), bd_ref[...], preferred_element_type=F32)
    for cc in range(SSM_STATE // S5_COLS):
        re = slice(cc * S5_COLS, (cc + 1) * S5_COLS)
        im = slice(SSM_STATE + cc * S5_COLS, SSM_STATE + (cc + 1) * S5_COLS)
        ar = a_ref[:, re]
        ai = a_ref[:, im]

        def body(jj, carry, re=re, im=im, ar=ar, ai=ai):
            sr, si = carry
            j = (tj - 1 - jj) if reverse else jj
            rows = pl.ds(pl.multiple_of(j * SUBLANES, SUBLANES), SUBLANES)
            nr = ar * sr - ai * si + s_scr[rows, re]
            ni = ar * si + ai * sr + s_scr[rows, im]
            s_scr[rows, re] = nr
            s_scr[rows, im] = ni
            return nr, ni

        sr, si = lax.fori_loop(0, tj, body, (st_scr[:, re], st_scr[:, im]))
        st_scr[:, re] = sr
        st_scr[:, im] = si
    if emit_y:
        y_ref[...] = jnp.dot(s_scr[...].astype(BF16), cd_ref[...], preferred_element_type=F32)
    else:
        y_ref[...] = jnp.zeros_like(y_ref)
    ends_ref[...] = st_scr[...]


def s5_pass(u_perm, bd, cd, a_t, init, reverse, emit_y):
    T, C = u_perm.shape
    tt = min(T, 512)
    nt = T // tt
    idx = (lambda i: (nt - 1 - i, 0)) if reverse else (lambda i: (i, 0))
    kern = functools.partial(_s5_kernel, reverse=reverse, tj=tt // SUBLANES, emit_y=emit_y)
    yt = tt if emit_y else SUBLANES
    return pl.pallas_call(
        kern,
        grid=(nt,),
        in_specs=[pl.BlockSpec((tt, C), idx),
                  pl.BlockSpec((C, 2 * SSM_STATE), lambda i: (0, 0)),
                  pl.BlockSpec((2 * SSM_STATE, C), lambda i: (0, 0)),
                  pl.BlockSpec((SUBLANES, 2 * SSM_STATE), lambda i: (0, 0)),
                  pl.BlockSpec((SUBLANES, 2 * SSM_STATE), lambda i: (0, 0))],
        out_specs=[pl.BlockSpec((yt, C), idx if emit_y else (lambda i: (0, 0))),
                   pl.BlockSpec((SUBLANES, 2 * SSM_STATE), lambda i: (0, 0))],
        out_shape=[jax.ShapeDtypeStruct((T if emit_y else SUBLANES, C), F32),
                   jax.ShapeDtypeStruct((SUBLANES, 2 * SSM_STATE), F32)],
        scratch_shapes=[pltpu.VMEM((tt, 2 * SSM_STATE), F32),
                        pltpu.VMEM((SUBLANES, 2 * SSM_STATE), F32)],
        compiler_params=_cparams("arbitrary"),
        name="s5_scan",
    )(u_perm, bd, cd, a_t, init)


def _cmul(ar, ai, br, bi):
    return ar * br - ai * bi, ar * bi + ai * br


def s5_direction(u_perm, bd, cd, ab_re, ab_im, s0, reverse):
    T = u_perm.shape[0]
    tseg = T // SUBLANES
    a_t = jnp.broadcast_to(jnp.concatenate([ab_re, ab_im])[None, :], (SUBLANES, 2 * SSM_STATE))
    _, ends = s5_pass(u_perm, bd, cd, a_t, jnp.zeros((SUBLANES, 2 * SSM_STATE), F32), reverse, False)
    pr, pi = ab_re, ab_im
    for _ in range(int(math.log2(tseg))):
        pr, pi = _cmul(pr, pi, pr, pi)
    er, ei = ends[:, :SSM_STATE], ends[:, SSM_STATE:]
    cr, ci = s0
    starts = [None] * SUBLANES
    order = range(SUBLANES - 1, -1, -1) if reverse else range(SUBLANES)
    for r in order:
        starts[r] = jnp.concatenate([cr, ci])
        nr, ni = _cmul(pr, pi, cr, ci)
        cr, ci = nr + er[r], ni + ei[r]
    y, _ = s5_pass(u_perm, bd, cd, a_t, jnp.stack(starts), reverse, True)
    return y, (cr, ci)


def s5_params(lam_re, lam_im, log_dt, b_re, b_im, c_re, c_im):
    lam_re = jnp.minimum(lam_re, -1e-4)
    dt = jnp.exp(log_dt)[:, None]
    mag = jnp.exp(lam_re * dt)
    ab_re = mag * jnp.cos(lam_im * dt)
    ab_im = mag * jnp.sin(lam_im * dt)
    den = lam_re * lam_re + lam_im * lam_im
    f_re = ((ab_re - 1) * lam_re + ab_im * lam_im) / den
    f_im = (ab_im * lam_re - (ab_re - 1) * lam_im) / den
    bb_re = f_re[..., None] * b_re - f_im[..., None] * b_im
    bb_im = f_re[..., None] * b_im + f_im[..., None] * b_re
    eye = jnp.eye(SSM_G, dtype=F32)
    blk_in = lambda t: jnp.einsum('gph,gk->ghkp', t, eye).reshape(SSM_CH, SSM_STATE)
    blk_out = lambda t: jnp.einsum('ghp,gk->gpkh', t, eye).reshape(SSM_STATE, SSM_CH)
    bd = jnp.concatenate([blk_in(bb_re), blk_in(bb_im)], axis=1).astype(BF16)
    cd = jnp.concatenate([blk_out(c_re), -blk_out(c_im)], axis=0).astype(BF16)
    return bd, cd, ab_re.reshape(-1), ab_im.reshape(-1)


def _to_segments(u):
    T, C = u.shape
    return u.reshape(SUBLANES, T // SUBLANES, C).transpose(1, 0, 2).reshape(T, C)


def _from_segments(y):
    T, C = y.shape
    return y.reshape(T // SUBLANES, SUBLANES, C).transpose(1, 0, 2).reshape(T, C)


def _conv3(x, w_ref, width):
    tm = x.shape[0]
    pos = lax.broadcasted_iota(jnp.int32, (tm, 1), 0) % width
    prev = jnp.where(pos == 0, 0.0, pltpu.roll(x, 1, 0))
    nxt = jnp.where(pos == width - 1, 0.0, pltpu.roll(x, tm - 1, 0))
    return prev * w_ref[0:1, :] + x * w_ref[1:2, :] + nxt * w_ref[2:3, :]


def _even_post_kernel(p_ref, yf_ref, yb_ref, h_ref, cw_ref, d_ref, gw_ref, gb_ref, wo_ref, m_ref,
                      o_ref, *, width):
    u = p_ref[:, 0:CONV_CH]
    g_in = p_ref[:, CONV_CH:2 * CONV_CH]
    g_out = p_ref[:, 2 * CONV_CH:3 * CONV_CH]
    branch_a = g_out * _conv3(g_in * u, cw_ref, width)
    s = p_ref[:, 3 * CONV_CH:] * d_ref[...] + yf_ref[...] + yb_ref[...]
    y = jax.nn.gelu(s)
    gate = jnp.dot(y.astype(BF16), gw_ref[...], preferred_element_type=F32) + gb_ref[...]
    branch_b = y * jax.nn.sigmoid(gate)
    out = (jnp.dot(branch_a.astype(BF16), wo_ref[0:CONV_CH, :], preferred_element_type=F32)
           + jnp.dot(branch_b.astype(BF16), wo_ref[CONV_CH:, :], preferred_element_type=F32))
    o_ref[...] = h_ref[...] + m_ref[...] * out


def even_post(p, yf, yb, h, conv_w, d_skip, glu_w_bf, glu_b, w_out_bf, gate, width):
    T, D = h.shape
    tm = min(T, 512)
    full = lambda shape: pl.BlockSpec(shape, lambda i: (0, 0))
    rows = lambda n: pl.BlockSpec((tm, n), lambda i: (i, 0))
    return pl.pallas_call(
        functools.partial(_even_post_kernel, width=width),
        grid=(T // tm,),
        in_specs=[rows(4 * CONV_CH), rows(SSM_CH), rows(SSM_CH), rows(D),
                  full((3, CONV_CH)), full((1, SSM_CH)), full((SSM_CH, SSM_CH)), full((1, SSM_CH)),
                  full((2 * CONV_CH, D)), full((1, D))],
        out_specs=rows(D),
        out_shape=jax.ShapeDtypeStruct((T, D), F32),
        compiler_params=_cparams("parallel"),
        name="even_post",
    )(p, yf, yb, h, conv_w, d_skip.reshape(1, -1), glu_w_bf, glu_b.reshape(1, -1), w_out_bf,
      gate.reshape(1, D))


def _extract_top(w, n):
    vals = []
    for _ in range(n):
        m = jnp.max(w, axis=0, keepdims=True)
        vals.append(m)
        w = jnp.where(w == m, -jnp.inf, w)
    return vals


def _peer_topk_kernel(q_ref, sk_ref, s0_ref, s1_ref, e0_ref, e1_ref, tau_ref, cand_scr):
    tq = q_ref.shape[0]
    sub = lax.broadcasted_iota(jnp.int32, (PEER_TOPK, tq), 0)
    for h in range(PEER_HEADS):
        sc = []
        for s in range(2):
            c0 = (2 * h + s) * LANES
            sc.append(lax.dot_general(sk_ref[s], q_ref[:, c0:c0 + LANES], NT_DIMS,
                                      precision=HI, preferred_element_type=F32))
        top0 = _extract_top(sc[0], PEER_TOPK)
        top1 = _extract_top(sc[1], PEER_TOPK)
        b = jnp.zeros((PEER_TOPK, tq), F32)
        for k in range(PEER_TOPK):
            b = jnp.where(sub == k, top1[k], b)
        for k in range(PEER_TOPK):
            cand_scr[k * PEER_TOPK:(k + 1) * PEER_TOPK, :] = top0[k] + b
        best = _extract_top(cand_scr[...], PEER_TOPK)
        z = jnp.zeros_like(best[0])
        for k in range(PEER_TOPK):
            z = z + jnp.exp(best[k] - best[0])
        s0_ref[h] = sc[0]
        s1_ref[h] = sc[1]
        e0_ref[h] = jnp.exp(sc[0] - top0[0]) / z
        e1_ref[h] = jnp.exp(sc[1] - top1[0])
        tau_ref[h:h + 1, :] = best[PEER_TOPK - 1]


def peer_topk(q, subkeys):
    T = q.shape[0]
    tq = min(T, 256)
    big = jax.ShapeDtypeStruct((PEER_HEADS, PEER_KEYS, T), F32)
    bspec = pl.BlockSpec((PEER_HEADS, PEER_KEYS, tq), lambda i: (0, 0, i))
    return pl.pallas_call(
        _peer_topk_kernel,
        grid=(T // tq,),
        in_specs=[pl.BlockSpec((tq, q.shape[1]), lambda i: (i, 0)),
                  pl.BlockSpec((2, PEER_KEYS, LANES), lambda i: (0, 0, 0))],
        out_specs=[bspec, bspec, bspec, bspec, pl.BlockSpec((PEER_HEADS, tq), lambda i: (0, i))],
        out_shape=[big, big, big, big, jax.ShapeDtypeStruct((PEER_HEADS, T), F32)],
        scratch_shapes=[pltpu.VMEM((PEER_TOPK * PEER_TOPK, tq), F32)],
        compiler_params=_cparams("parallel"),
        name="peer_topk",
    )(q, subkeys)


PEER_TE = 1024
PEER_JH = 64


def _peer_dense_kernel(x_ref, u_ref, vt_ref, s0_ref, e0_ref, s1_ref, e1_ref, tau_ref, h_ref, m_ref,
                       fg_ref, o_ref, acc_scr, st_scr, a_scr, *, final_norm):
    j = pl.program_id(1)
    tq = x_ref.shape[0]

    @pl.when(j == 0)
    def _():
        acc_scr[...] = jnp.zeros_like(acc_scr)

    st_scr[...] = lax.dot_general(u_ref[...], x_ref[...], NT_DIMS, preferred_element_type=F32)

    def col_body(c, carry):
        cs = pl.ds(pl.multiple_of(c * LANES, LANES), LANES)
        taus = [tau_ref[h:h + 1, cs] for h in range(PEER_HEADS)]
        for jh in range(PEER_KEYS // PEER_JH):
            js = slice(jh * PEER_JH, (jh + 1) * PEER_JH)
            for ii in range(PEER_TE // PEER_KEYS):
                w = jnp.zeros((PEER_JH, LANES), F32)
                for h in range(PEER_HEADS):
                    pair = s1_ref[h, js, cs] + s0_ref[h, ii:ii + 1, cs]
                    w = w + jnp.where(pair >= taus[h], e1_ref[h, js, cs] * e0_ref[h, ii:ii + 1, cs], 0.0)
                rows = slice(ii * PEER_KEYS + jh * PEER_JH, ii * PEER_KEYS + (jh + 1) * PEER_JH)
                a_scr[rows, cs] = (jax.nn.gelu(st_scr[rows, cs]) * w).astype(BF16)
        return carry

    lax.fori_loop(0, tq // LANES, col_body, 0)
    acc_scr[...] += jnp.dot(vt_ref[...], a_scr[...], preferred_element_type=F32)

    @pl.when(j == pl.num_programs(1) - 1)
    def _():
        out = h_ref[...] + m_ref[...] * acc_scr[...].T
        if final_norm:
            out = out * lax.rsqrt(jnp.mean(out * out, axis=-1, keepdims=True) + RMS_EPS) * fg_ref[...]
        o_ref[...] = out


def peer_dense(x_bf, u_bf, vt_bf, s0, s1, e0, e1, tau, h, gate, final_g):
    T, D = h.shape
    tq = min(T, 512)
    ni = PEER_TE // PEER_KEYS
    final_norm = final_g is not None
    fg = (final_g if final_norm else jnp.ones((D,), F32)).reshape(1, D)
    tok_rows = lambda dt: pl.BlockSpec((tq, D), lambda i, j: (i, 0))
    head_all = pl.BlockSpec((PEER_HEADS, PEER_KEYS, tq), lambda i, j: (0, 0, i))
    head_blk = pl.BlockSpec((PEER_HEADS, ni, tq), lambda i, j: (0, j, i))
    return pl.pallas_call(
        functools.partial(_peer_dense_kernel, final_norm=final_norm),
        grid=(T // tq, PEER_EXPERTS // PEER_TE),
        in_specs=[tok_rows(BF16),
                  pl.BlockSpec((PEER_TE, D), lambda i, j: (j, 0)),
                  pl.BlockSpec((D, PEER_TE), lambda i, j: (0, j)),
                  head_blk, head_blk, head_all, head_all,
                  pl.BlockSpec((PEER_HEADS, tq), lambda i, j: (0, i)),
                  tok_rows(F32),
                  pl.BlockSpec((1, D), lambda i, j: (0, 0)),
                  pl.BlockSpec((1, D), lambda i, j: (0, 0))],
        out_specs=pl.BlockSpec((tq, D), lambda i, j: (i, 0)),
        out_shape=jax.ShapeDtypeStruct((T, D), F32),
        scratch_shapes=[pltpu.VMEM((D, tq), F32),
                        pltpu.VMEM((PEER_TE, tq), F32),
                        pltpu.VMEM((PEER_TE, tq), BF16)],
        compiler_params=_cparams("parallel", "arbitrary"),
        name="peer_dense",
    )(x_bf, u_bf, vt_bf, s0, e0, s1, e1, tau, h, gate.reshape(1, D), fg)


def peer_block(h, norm_g, shift, scale, gate, wq_bf, subkeys, u_bf, vt_bf, final_g=None):
    q, x_bf = nmm(h, norm_g, shift, scale, wq_bf, tn=1024)
    s0, s1, e0, e1, tau = peer_topk(q, subkeys)
    return peer_dense(x_bf, u_bf, vt_bf, s0, s1, e0, e1, tau, h, gate, final_g)


def _softplus(x):
    return jnp.maximum(x, 0.0) + jnp.log(1.0 + jnp.exp(-jnp.abs(x)))


def _gdn_prep_kernel(p_ref, cw_ref, alog_ref, dtb_ref, q_ref, k_ref, v_ref, g_ref, *, width):
    tm = p_ref.shape[0]
    pos = lax.broadcasted_iota(jnp.int32, (tm, 1), 0) % width
    first = pos == 0
    last = pos == width - 1
    for cb in range(DN_CONV_CH // LANES):
        cs = slice(cb * LANES, (cb + 1) * LANES)
        x = p_ref[:, cs]
        prev = jnp.where(first, 0.0, pltpu.roll(x, 1, 0))
        nxt = jnp.where(last, 0.0, pltpu.roll(x, tm - 1, 0))
        c = _silu(prev * cw_ref[0:1, cs] + x * cw_ref[1:2, cs] + nxt * cw_ref[2:3, cs])
        hs = slice((cb % DN_HEADS) * LANES, (cb % DN_HEADS + 1) * LANES)
        if cb < DN_HEADS:
            q_ref[:, hs] = c * (lax.rsqrt(jnp.sum(c * c, axis=-1, keepdims=True) + 1e-6) * DN_DK ** -0.5)
        elif cb < 2 * DN_HEADS:
            k_ref[:, hs] = c * lax.rsqrt(jnp.sum(c * c, axis=-1, keepdims=True) + 1e-6)
        else:
            v_ref[:, hs] = c
    ab = p_ref[:, DN_IN_PAD - LANES:DN_IN_PAD]
    lane = lax.broadcasted_iota(jnp.int32, (1, LANES), 1)
    g = -jnp.exp(alog_ref[...]) * _softplus(ab + dtb_ref[...])
    g_ref[...] = jnp.where(lane < 2 * DN_HEADS, g, jnp.where(lane < 4 * DN_HEADS, jax.nn.sigmoid(ab), 0.0))


def gdn_prep(p, conv_w, a_log, dt_bias, width):
    T = p.shape[0]
    tm = 256
    pad = lambda v: jnp.zeros((1, LANES), F32).at[0, :2 * DN_HEADS].set(v.reshape(-1))
    qkv = jax.ShapeDtypeStruct((T, DN_HEADS * DN_DK), F32)
    rows = lambda n: pl.BlockSpec((tm, n), lambda i: (i, 0))
    return pl.pallas_call(
        functools.partial(_gdn_prep_kernel, width=width),
        grid=(T // tm,),
        in_specs=[rows(DN_IN_PAD),
                  pl.BlockSpec((3, DN_CONV_CH), lambda i: (0, 0)),
                  pl.BlockSpec((1, LANES), lambda i: (0, 0)),
                  pl.BlockSpec((1, LANES), lambda i: (0, 0))],
        out_specs=[rows(DN_HEADS * DN_DK)] * 3 + [rows(LANES)],
        out_shape=[qkv, qkv, qkv, jax.ShapeDtypeStruct((T, LANES), F32)],
        compiler_params=_cparams("parallel"),
        name="gdn_prep",
    )(p, conv_w, pad(a_log), pad(dt_bias))


def _gdn_chunk_kernel(qf_ref, kf_ref, vf_ref, gf_ref, qb_ref, kb_ref, vb_ref, gb_ref, s0_ref,
                      of_ref, ob_ref, sfin_ref, s_scr):
    C = DN_CHUNK

    @pl.when(pl.program_id(0) == 0)
    def _():
        s_scr[...] = s0_ref[...]

    row = lax.broadcasted_iota(jnp.int32, (C, C), 0)
    col = lax.broadcasted_iota(jnp.int32, (C, C), 1)
    eye = (row == col).astype(F32)
    ones = jnp.ones((C, C), F32)
    streams = ((qf_ref, kf_ref, vf_ref, gf_ref, of_ref), (qb_ref, kb_ref, vb_ref, gb_ref, ob_ref))
    for d, (q_ref, k_ref, v_ref, g_ref, o_ref) in enumerate(streams):
        inc = (col <= row) if d == 0 else (col >= row)
        strict = (col < row) if d == 0 else (col > row)
        inc_t = (row <= col) if d == 0 else (row >= col)
        inc_f = inc.astype(F32)
        for h in range(DN_HEADS):
            hs = slice(h * DN_DK, (h + 1) * DN_DK)
            q = q_ref[:, hs]
            k = k_ref[:, hs]
            v = v_ref[:, hs]
            gl = DN_HEADS * d + h
            g_b = jnp.broadcast_to(g_ref[:, gl:gl + 1], (C, LANES))
            beta_b = jnp.broadcast_to(g_ref[:, 2 * DN_HEADS + gl:2 * DN_HEADS + gl + 1], (C, LANES))
            gc = jnp.dot(inc_f, g_b, precision=HI, preferred_element_type=F32)
            g_row = jnp.dot(ones, jnp.where(inc_t, g_b[:, :C], 0.0), precision=HI,
                            preferred_element_type=F32)
            decay = jnp.where(inc, jnp.exp(jnp.where(inc, gc[:, :C] - g_row, 0.0)), 0.0)
            g_tot = jnp.sum(g_b, axis=0, keepdims=True)
            k_beta = k * beta_b
            v_beta = v * beta_b
            k_bf = k.astype(BF16)
            kk = lax.dot_general(k_beta.astype(BF16), k_bf, NT_DIMS, preferred_element_type=F32)
            a = jnp.where(strict, kk * decay, 0.0)
            t_inv = eye - a
            p = a
            for _ in range(5):
                p_bf = p.astype(BF16)
                p = jnp.dot(p_bf, p_bf, preferred_element_type=F32)
                t_inv = t_inv + jnp.dot(t_inv.astype(BF16), p.astype(BF16), preferred_element_type=F32)
            e_gc = jnp.exp(gc)
            rhs = jnp.concatenate([v_beta, k_beta * e_gc], axis=1).astype(BF16)
            uw = jnp.dot(t_inv.astype(BF16), rhs, preferred_element_type=F32)
            u = uw[:, :DN_DK]
            w = uw[:, DN_DK:]
            attn = lax.dot_general(q.astype(BF16), k_bf, NT_DIMS, preferred_element_type=F32) * decay
            s = s_scr[d, h]
            ws = jnp.dot(jnp.concatenate([w, q * e_gc], axis=0).astype(BF16), s.astype(BF16),
                         preferred_element_type=F32)
            v_new = u - ws[:C]
            v_new_bf = v_new.astype(BF16)
            o_ref[:, hs] = ws[C:] + jnp.dot(attn.astype(BF16), v_new_bf, preferred_element_type=F32)
            k_dec = (k * jnp.exp(g_tot - gc)).astype(BF16)
            s_scr[d, h] = s * jnp.exp(g_tot) + lax.dot_general(k_dec, v_new_bf, TN_DIMS,
                                                               preferred_element_type=F32)
    sfin_ref[...] = s_scr[...]


def gdn_chunk(q, k, v, g, s0):
    T, HD = q.shape
    n = T // DN_CHUNK
    fwd = lambda i: (i, 0)
    bwd = lambda i: (n - 1 - i, 0)
    qkv = lambda m: pl.BlockSpec((DN_CHUNK, HD), m)
    gsp = lambda m: pl.BlockSpec((DN_CHUNK, LANES), m)
    sspec = pl.BlockSpec((2, DN_HEADS, DN_DK, DN_DK), lambda i: (0, 0, 0, 0))
    return pl.pallas_call(
        _gdn_chunk_kernel,
        grid=(n,),
        in_specs=[qkv(fwd), qkv(fwd), qkv(fwd), gsp(fwd), qkv(bwd), qkv(bwd), qkv(bwd), gsp(bwd), sspec],
        out_specs=[qkv(fwd), qkv(bwd), sspec],
        out_shape=[jax.ShapeDtypeStruct((T, HD), F32), jax.ShapeDtypeStruct((T, HD), F32),
                   jax.ShapeDtypeStruct((2, DN_HEADS, DN_DK, DN_DK), F32)],
        scratch_shapes=[pltpu.VMEM((2, DN_HEADS, DN_DK, DN_DK), F32)],
        compiler_params=_cparams("arbitrary"),
        name="gdn_chunk",
    )(q, k, v, g, q, k, v, g, s0)


def _gdn_out_kernel(of_ref, ob_ref, z_ref, ng_ref, w_ref, h_ref, m_ref, o_ref):
    ys = []
    for h in range(DN_HEADS):
        hs = slice(h * DN_DK, (h + 1) * DN_DK)
        o = of_ref[:, hs] + ob_ref[:, hs]
        y = o * lax.rsqrt(jnp.mean(o * o, axis=-1, keepdims=True) + RMS_EPS) * ng_ref[...]
        ys.append((y * _silu(z_ref[:, hs])).astype(BF16))
    out = jnp.dot(jnp.concatenate(ys, axis=1), w_ref[...], preferred_element_type=F32)
    o_ref[...] = h_ref[...] + m_ref[...] * out


def gdn_out(o_f, o_b, p, norm_g, w_out_bf, h, gate):
    T, D = h.shape
    tm = min(T, 512)
    rows = lambda n: pl.BlockSpec((tm, n), lambda i: (i, 0))
    return pl.pallas_call(
        _gdn_out_kernel,
        grid=(T // tm,),
        in_specs=[rows(D), rows(D),
                  pl.BlockSpec((tm, D), lambda i: (i, DN_CONV_CH // D)),
                  pl.BlockSpec((1, DN_DK), lambda i: (0, 0)),
                  pl.BlockSpec((D, D), lambda i: (0, 0)),
                  rows(D),
                  pl.BlockSpec((1, D), lambda i: (0, 0))],
        out_specs=rows(D),
        out_shape=jax.ShapeDtypeStruct((T, D), F32),
        compiler_params=_cparams("parallel"),
        name="gdn_out",
    )(o_f, o_b, p, norm_g.reshape(1, -1), w_out_bf, h, gate.reshape(1, D))


def kernel(x, c, ctx, c_ctx, ada_w, ada_b, norm1_g, norm2_g, e_w_in, e_conv_w, s5_lam_re, s5_lam_im,
           s5_log_dt, s5_b_re, s5_b_im, s5_c_re, s5_c_im, s5_d, s5_glu_w, s5_glu_b, e_w_out, o_w_in,
           dn_conv_w, dn_a_log, dn_dt_bias, dn_norm_g, o_w_out, peer_wq, peer_subkeys, peer_u, peer_v,
           final_g):
    D = D_MODEL
    h_lat = x[0]
    h_ctx = ctx[0]
    ctx_len = h_ctx.shape[0]
    cv = jnp.zeros((SUBLANES, D), F32).at[0].set(c[0]).at[1].set(c_ctx)
    mods = adaln(cv, ada_w, ada_b)
    depth = ada_w.shape[0]
    for l in range(depth):
        last = l == depth - 1
        i = l // 2
        m_lat = mods[l, 0].reshape(N_MOD, D)
        m_ctx = mods[l, 1].reshape(N_MOD, D)
        if l % 2 == 0:
            w_in = e_w_in[i].astype(BF16)
            p_lat, _ = nmm(h_lat, norm1_g[l], m_lat[0], m_lat[1], w_in, tn=1024)
            p_ctx, _ = nmm(h_ctx, norm1_g[l], m_ctx[0], m_ctx[1], w_in, tn=1024)
            u_lat = _to_segments(p_lat[:, 3 * CONV_CH:])
            u_ctx = _to_segments(p_ctx[:, 3 * CONV_CH:])
            y_lat, y_ctx = [], []
            zero = (jnp.zeros((SSM_STATE,), F32), jnp.zeros((SSM_STATE,), F32))
            for di in range(2):
                bd, cd, ab_re, ab_im = s5_params(s5_lam_re[i, di], s5_lam_im[i, di], s5_log_dt[i, di],
                                                 s5_b_re[i, di], s5_b_im[i, di], s5_c_re[i, di],
                                                 s5_c_im[i, di])
                yc, fin = s5_direction(u_ctx, bd, cd, ab_re, ab_im, zero, di == 1)
                yl, _ = s5_direction(u_lat, bd, cd, ab_re, ab_im, fin, di == 1)
                y_ctx.append(_from_segments(yc))
                y_lat.append(_from_segments(yl))
            glu_w = s5_glu_w[i].astype(BF16)
            w_out = e_w_out[i].astype(BF16)
            h_lat = even_post(p_lat, y_lat[0], y_lat[1], h_lat, e_conv_w[i], s5_d[i], glu_w, s5_glu_b[i],
                              w_out, m_lat[2], GRID_W)
            if not last:
                h_ctx = even_post(p_ctx, y_ctx[0], y_ctx[1], h_ctx, e_conv_w[i], s5_d[i], glu_w,
                                  s5_glu_b[i], w_out, m_ctx[2], ctx_len)
        else:
            w_in = jnp.pad(o_w_in[i], ((0, 0), (0, DN_IN_PAD - o_w_in.shape[2]))).astype(BF16)
            p_lat, _ = nmm(h_lat, norm1_g[l], m_lat[0], m_lat[1], w_in, tn=DN_IN_PAD // 3)
            p_ctx, _ = nmm(h_ctx, norm1_g[l], m_ctx[0], m_ctx[1], w_in, tn=DN_IN_PAD // 3)
            qc, kc, vc, gc = gdn_prep(p_ctx, dn_conv_w[i], dn_a_log[i], dn_dt_bias[i], ctx_len)
            ql, kl, vl, gl = gdn_prep(p_lat, dn_conv_w[i], dn_a_log[i], dn_dt_bias[i], GRID_W)
            zero = jnp.zeros((2, DN_HEADS, DN_DK, DN_DK), F32)
            oc_f, oc_b, s_ctx = gdn_chunk(qc, kc, vc, gc, zero)
            ol_f, ol_b, _ = gdn_chunk(ql, kl, vl, gl, s_ctx)
            w_out = o_w_out[i].astype(BF16)
            h_lat = gdn_out(ol_f, ol_b, p_lat, dn_norm_g[i], w_out, h_lat, m_lat[2])
            if not last:
                h_ctx = gdn_out(oc_f, oc_b, p_ctx, dn_norm_g[i], w_out, h_ctx, m_ctx[2])
        wq = peer_wq[l].astype(BF16)
        u_bf = peer_u[l].astype(BF16)
        vt_bf = peer_v[l].T.astype(BF16)
        h_lat = peer_block(h_lat, norm2_g[l], m_lat[3], m_lat[4], m_lat[5], wq, peer_subkeys[l], u_bf,
                           vt_bf, final_g if last else None)
        if not last:
            h_ctx = peer_block(h_ctx, norm2_g[l], m_ctx[3], m_ctx[4], m_ctx[5], wq, peer_subkeys[l], u_bf,
                               vt_bf)
    return h_lat[None]
```

```python
import functools
import math

import jax
import jax.numpy as jnp
from jax import lax
from jax.experimental import pallas as pl
from jax.experimental.pallas import tpu as pltpu

F32 = jnp.float32
BF16 = jnp.bfloat16
HI = lax.Precision.HIGHEST

D_MODEL = 1024
GRID_W = 64
RMS_EPS = 1e-6
N_MOD = 6
CONV_CH = 512
SSM_CH = 512
SSM_H = 16
SSM_G = 32
SSM_P = 64
SSM_STATE = SSM_G * SSM_P
DN_HEADS = 8
DN_DK = 128
DN_CHUNK = 64
DN_CONV_CH = 3072
DN_IN_PAD = 4224
PEER_HEADS = 8
PEER_KEYS = 128
PEER_TOPK = 16
PEER_EXPERTS = PEER_KEYS * PEER_KEYS

SUBLANES = 8
LANES = 128
VMEM_LIMIT = 52 * 1024 * 1024

NT_DIMS = (((1,), (1,)), ((), ()))
TN_DIMS = (((0,), (0,)), ((), ()))


def _cparams(*sem):
    return pltpu.CompilerParams(dimension_semantics=sem, vmem_limit_bytes=VMEM_LIMIT)


def _silu(x):
    return x * jax.nn.sigmoid(x)


def _adaln_kernel(c_ref, w_ref, b_ref, o_ref):
    sc = _silu(c_ref[...])
    o_ref[0] = jnp.dot(sc, w_ref[0], precision=HI, preferred_element_type=F32) + b_ref[0]


def adaln(cv, ada_w, ada_b):
    L, D, N = ada_w.shape
    tn = 1536
    return pl.pallas_call(
        _adaln_kernel,
        grid=(L, N // tn),
        in_specs=[pl.BlockSpec((SUBLANES, D), lambda l, j: (0, 0)),
                  pl.BlockSpec((1, D, tn), lambda l, j: (l, 0, j)),
                  pl.BlockSpec((1, 1, tn), lambda l, j: (l, 0, j))],
        out_specs=pl.BlockSpec((1, SUBLANES, tn), lambda l, j: (l, 0, j)),
        out_shape=jax.ShapeDtypeStruct((L, SUBLANES, N), F32),
        compiler_params=_cparams("parallel", "parallel"),
        name="adaln",
    )(cv, ada_w, ada_b.reshape(L, 1, N))


def _nmm_kernel(x_ref, g_ref, sh_ref, sc_ref, w_ref, o_ref, a_ref):
    @pl.when(pl.program_id(1) == 0)
    def _():
        x = x_ref[...]
        y = x * lax.rsqrt(jnp.mean(x * x, axis=-1, keepdims=True) + RMS_EPS)
        a = (y * g_ref[...]) * (1.0 + sc_ref[...]) + sh_ref[...]
        a_ref[...] = a.astype(BF16)

    o_ref[...] = jnp.dot(a_ref[...], w_ref[...], preferred_element_type=F32)


def nmm(x, g, shift, scale, w_bf, tn):
    T, D = x.shape
    N = w_bf.shape[1]
    tm = min(T, 512)
    row = lambda v: v.reshape(1, D)
    return pl.pallas_call(
        _nmm_kernel,
        grid=(T // tm, N // tn),
        in_specs=[pl.BlockSpec((tm, D), lambda i, j: (i, 0)),
                  pl.BlockSpec((1, D), lambda i, j: (0, 0)),
                  pl.BlockSpec((1, D), lambda i, j: (0, 0)),
                  pl.BlockSpec((1, D), lambda i, j: (0, 0)),
                  pl.BlockSpec((D, tn), lambda i, j: (0, j))],
        out_specs=[pl.BlockSpec((tm, tn), lambda i, j: (i, j)),
                   pl.BlockSpec((tm, D), lambda i, j: (i, 0))],
        out_shape=[jax.ShapeDtypeStruct((T, N), F32),
                   jax.ShapeDtypeStruct((T, D), BF16)],
        compiler_params=_cparams("parallel", "arbitrary"),
        name="nmm",
    )(x, row(g), row(shift), row(scale), w_bf)


S5_COLS = 512


def _s5_kernel(u_ref, bd_ref, cd_ref, a_ref, init_ref, y_ref, ends_ref, s_scr, st_scr,
               *, reverse, tj, emit_y):
    @pl.when(pl.program_id(0) == 0)
    def _():
        st_scr[...] = init_ref[...]

    s_scr[...] = jnp.dot(u_ref[...].astype(BF16), bd_ref[...], preferred_element_type=F32)
    for cc in range(SSM_STATE // S5_COLS):
        re = slice(cc * S5_COLS, (cc + 1) * S5_COLS)
        im = slice(SSM_STATE + cc * S5_COLS, SSM_STATE + (cc + 1) * S5_COLS)
        ar = a_ref[:, re]
        ai = a_ref[:, im]

        def body(jj, carry, re=re, im=im, ar=ar, ai=ai):
            sr, si = carry
            j = (tj - 1 - jj) if reverse else jj
            rows = pl.ds(pl.multiple_of(j * SUBLANES, SUBLANES), SUBLANES)
            nr = ar * sr - ai * si + s_scr[rows, re]
            ni = ar * si + ai * sr + s_scr[rows, im]
            s_scr[rows, re] = nr
            s_scr[rows, im] = ni
            return nr, ni

        sr, si = lax.fori_loop(0, tj, body, (st_scr[:, re], st_scr[:, im]))
        st_scr[:, re] = sr
        st_scr[:, im] = si
    if emit_y:
        y_ref[...] = jnp.dot(s_scr[...].astype(BF16), cd_ref[...], preferred_element_type=F32)
    else:
        y_ref[...] = jnp.zeros_like(y_ref)
    ends_ref[...] = st_scr[...]


def s5_pass(u_perm, bd, cd, a_t, init, reverse, emit_y):
    T, C = u_perm.shape
    tt = min(T, 512)
    nt = T // tt
    idx = (lambda i: (nt - 1 - i, 0)) if reverse else (lambda i: (i, 0))
    kern = functools.partial(_s5_kernel, reverse=reverse, tj=tt // SUBLANES, emit_y=emit_y)
    yt = tt if emit_y else SUBLANES
    return pl.pallas_call(
        kern,
        grid=(nt,),
        in_specs=[pl.BlockSpec((tt, C), idx),
                  pl.BlockSpec((C, 2 * SSM_STATE), lambda i: (0, 0)),
                  pl.BlockSpec((2 * SSM_STATE, C), lambda i: (0, 0)),
                  pl.BlockSpec((SUBLANES, 2 * SSM_STATE), lambda i: (0, 0)),
                  pl.BlockSpec((SUBLANES, 2 * SSM_STATE), lambda i: (0, 0))],
        out_specs=[pl.BlockSpec((yt, C), idx if emit_y else (lambda i: (0, 0))),
                   pl.BlockSpec((SUBLANES, 2 * SSM_STATE), lambda i: (0, 0))],
        out_shape=[jax.ShapeDtypeStruct((T if emit_y else SUBLANES, C), F32),
                   jax.ShapeDtypeStruct((SUBLANES, 2 * SSM_STATE), F32)],
        scratch_shapes=[pltpu.VMEM((tt, 2 * SSM_STATE), F32),
                        pltpu.VMEM((SUBLANES, 2 * SSM_STATE), F32)],
        compiler_params=_cparams("arbitrary"),
        name="s5_scan",
    )(u_perm, bd, cd, a_t, init)


def _cmul(ar, ai, br, bi):
    return ar * br - ai * bi, ar * bi + ai * br


def s5_direction(u_perm, bd, cd, ab_re, ab_im, s0, reverse):
    T = u_perm.shape[0]
    tseg = T // SUBLANES
    a_t = jnp.broadcast_to(jnp.concatenate([ab_re, ab_im])[None, :], (SUBLANES, 2 * SSM_STATE))
    _, ends = s5_pass(u_perm, bd, cd, a_t, jnp.zeros((SUBLANES, 2 * SSM_STATE), F32), reverse, False)
    pr, pi = ab_re, ab_im
    for _ in range(int(math.log2(tseg))):
        pr, pi = _cmul(pr, pi, pr, pi)
    er, ei = ends[:, :SSM_STATE], ends[:, SSM_STATE:]
    cr, ci = s0
    starts = [None] * SUBLANES
    order = range(SUBLANES - 1, -1, -1) if reverse else range(SUBLANES)
    for r in order:
        starts[r] = jnp.concatenate([cr, ci])
        nr, ni = _cmul(pr, pi, cr, ci)
        cr, ci = nr + er[r], ni + ei[r]
    y, _ = s5_pass(u_perm, bd, cd, a_t, jnp.stack(starts), reverse, True)
    return y, (cr, ci)


def s5_params(lam_re, lam_im, log_dt, b_re, b_im, c_re, c_im):
    lam_re = jnp.minimum(lam_re, -1e-4)
    dt = jnp.exp(log_dt)[:, None]
    mag = jnp.exp(lam_re * dt)
    ab_re = mag * jnp.cos(lam_im * dt)
    ab_im = mag * jnp.sin(lam_im * dt)
    den = lam_re * lam_re + lam_im * lam_im
    f_re = ((ab_re - 1) * lam_re + ab_im * lam_im) / den
    f_im = (ab_im * lam_re - (ab_re - 1) * lam_im) / den
    bb_re = f_re[..., None] * b_re - f_im[..., None] * b_im
    bb_im = f_re[..., None] * b_im + f_im[..., None] * b_re
    eye = jnp.eye(SSM_G, dtype=F32)
    blk_in = lambda t: jnp.einsum('gph,gk->ghkp', t, eye).reshape(SSM_CH, SSM_STATE)
    blk_out = lambda t: jnp.einsum('ghp,gk->gpkh', t, eye).reshape(SSM_STATE, SSM_CH)
    bd = jnp.concatenate([blk_in(bb_re), blk_in(bb_im)], axis=1).astype(BF16)
    cd = jnp.concatenate([blk_out(c_re), -blk_out(c_im)], axis=0).astype(BF16)
    return bd, cd, ab_re.reshape(-1), ab_im.reshape(-1)


def _to_segments(u):
    T, C = u.shape
    return u.reshape(SUBLANES, T // SUBLANES, C).transpose(1, 0, 2).reshape(T, C)


def _from_segments(y):
    T, C = y.shape
    return y.reshape(T // SUBLANES, SUBLANES, C).transpose(1, 0, 2).reshape(T, C)


def _conv3(x, w_ref, width):
    tm = x.shape[0]
    pos = lax.broadcasted_iota(jnp.int32, (tm, 1), 0) % width
    prev = jnp.where(pos == 0, 0.0, pltpu.roll(x, 1, 0))
    nxt = jnp.where(pos == width - 1, 0.0, pltpu.roll(x, tm - 1, 0))
    return prev * w_ref[0:1, :] + x * w_ref[1:2, :] + nxt * w_ref[2:3, :]


def _even_post_kernel(p_ref, yf_ref, yb_ref, h_ref, cw_ref, d_ref, gw_ref, gb_ref, wo_ref, m_ref,
                      o_ref, *, width):
    u = p_ref[:, 0:CONV_CH]
    g_in = p_ref[:, CONV_CH:2 * CONV_CH]
    g_out = p_ref[:, 2 * CONV_CH:3 * CONV_CH]
    branch_a = g_out * _conv3(g_in * u, cw_ref, width)
    s = p_ref[:, 3 * CONV_CH:] * d_ref[...] + yf_ref[...] + yb_ref[...]
    y = jax.nn.gelu(s)
    gate = jnp.dot(y.astype(BF16), gw_ref[...], preferred_element_type=F32) + gb_ref[...]
    branch_b = y * jax.nn.sigmoid(gate)
    out = (jnp.dot(branch_a.astype(BF16), wo_ref[0:CONV_CH, :], preferred_element_type=F32)
           + jnp.dot(branch_b.astype(BF16), wo_ref[CONV_CH:, :], preferred_element_type=F32))
    o_ref[...] = h_ref[...] + m_ref[...] * out


def even_post(p, yf, yb, h, conv_w, d_skip, glu_w_bf, glu_b, w_out_bf, gate, width):
    T, D = h.shape
    tm = min(T, 512)
    full = lambda shape: pl.BlockSpec(shape, lambda i: (0, 0))
    rows = lambda n: pl.BlockSpec((tm, n), lambda i: (i, 0))
    return pl.pallas_call(
        functools.partial(_even_post_kernel, width=width),
        grid=(T // tm,),
        in_specs=[rows(4 * CONV_CH), rows(SSM_CH), rows(SSM_CH), rows(D),
                  full((3, CONV_CH)), full((1, SSM_CH)), full((SSM_CH, SSM_CH)), full((1, SSM_CH)),
                  full((2 * CONV_CH, D)), full((1, D))],
        out_specs=rows(D),
        out_shape=jax.ShapeDtypeStruct((T, D), F32),
        compiler_params=_cparams("parallel"),
        name="even_post",
    )(p, yf, yb, h, conv_w, d_skip.reshape(1, -1), glu_w_bf, glu_b.reshape(1, -1), w_out_bf,
      gate.reshape(1, D))


def _extract_top(w, n):
    vals = []
    for _ in range(n):
        m = jnp.max(w, axis=0, keepdims=True)
        vals.append(m)
        w = jnp.where(w == m, -jnp.inf, w)
    return vals


def _peer_topk_kernel(q_ref, sk_ref, t0_ref, e0_ref, s1_ref, e1_ref, cand_scr):
    tq = q_ref.shape[0]
    sub = lax.broadcasted_iota(jnp.int32, (PEER_TOPK, tq), 0)
    for h in range(PEER_HEADS):
        sc = []
        for s in range(2):
            c0 = (2 * h + s) * LANES
            sc.append(lax.dot_general(sk_ref[s], q_ref[:, c0:c0 + LANES], NT_DIMS,
                                      precision=HI, preferred_element_type=F32))
        top0 = _extract_top(sc[0], PEER_TOPK + 1)
        top1 = _extract_top(sc[1], PEER_TOPK + 1)
        b = jnp.zeros((PEER_TOPK, tq), F32)
        for k in range(PEER_TOPK):
            b = jnp.where(sub == k, top1[k], b)
        for k in range(PEER_TOPK):
            cand_scr[k * PEER_TOPK:(k + 1) * PEER_TOPK, :] = top0[k] + b
        best = _extract_top(cand_scr[...], PEER_TOPK + 1)
        z = jnp.zeros_like(best[0])
        for k in range(PEER_TOPK):
            z = z + jnp.exp(best[k] - best[0])
        runner_up = jnp.maximum(best[PEER_TOPK],
                                jnp.maximum(top0[0] + top1[PEER_TOPK], top0[PEER_TOPK] + top1[0]))
        t0 = 0.5 * (best[PEER_TOPK - 1] + runner_up) - sc[0]
        e0 = jnp.exp(sc[0] - top0[0]) / z
        e1 = jnp.exp(sc[1] - top1[0])
        for c in range(tq // LANES):
            cs = slice(c * LANES, (c + 1) * LANES)
            t0_ref[h, c] = t0[:, cs]
            e0_ref[h, c] = e0[:, cs]
            s1_ref[h, c] = sc[1][:, cs]
            e1_ref[h, c] = e1[:, cs]


def peer_topk(q, subkeys):
    T = q.shape[0]
    tq = min(T, 256)
    big = jax.ShapeDtypeStruct((PEER_HEADS, T // LANES, PEER_KEYS, LANES), F32)
    bspec = pl.BlockSpec((PEER_HEADS, tq // LANES, PEER_KEYS, LANES), lambda i: (0, i, 0, 0))
    return pl.pallas_call(
        _peer_topk_kernel,
        grid=(T // tq,),
        in_specs=[pl.BlockSpec((tq, q.shape[1]), lambda i: (i, 0)),
                  pl.BlockSpec((2, PEER_KEYS, LANES), lambda i: (0, 0, 0))],
        out_specs=[bspec, bspec, bspec, bspec],
        out_shape=[big, big, big, big],
        scratch_shapes=[pltpu.VMEM((PEER_TOPK * PEER_TOPK, tq), F32)],
        compiler_params=_cparams("parallel"),
        name="peer_topk",
    )(q, subkeys)


PEER_TE = 1024
PEER_JH = 16
PEER_IU = 4


def _peer_dense_kernel(x_ref, u_ref, vt_ref, t0_ref, e0_ref, s1_ref, e1_ref, h_ref, m_ref, fg_ref, o_ref,
                       acc_scr, st_a, st_b, a_a, a_b, *, final_norm):
    j = pl.program_id(1)
    tq, d = x_ref.shape
    ncol = tq // LANES
    e_rows = PEER_TE // ncol
    d_rows = d // ncol

    @pl.when(j == 0)
    def _():
        acc_scr[...] = jnp.zeros_like(acc_scr)
        st_a[...] = jnp.zeros_like(st_a)
        st_b[...] = jnp.zeros_like(st_b)
        a_a[...] = jnp.zeros_like(a_a)
        a_b[...] = jnp.zeros_like(a_b)

    def run(st_w, st_r, a_w, a_r):
        def body(c, carry):
            er = pl.ds(pl.multiple_of(c * e_rows, e_rows), e_rows)
            dr = pl.ds(pl.multiple_of(c * d_rows, d_rows), d_rows)
            half = tq // 2

            def mxu_piece(k):
                ts = slice((k % 2) * half, (k % 2 + 1) * half)
                if k < 2:
                    res = lax.dot_general(u_ref[er, :], x_ref[ts, :], NT_DIMS, preferred_element_type=F32)
                    st_w[er, ts] = res
                else:
                    res = acc_scr[dr, ts] + jnp.dot(vt_ref[dr, :], a_r[:, ts], preferred_element_type=F32)
                    acc_scr[dr, ts] = res
                return jnp.minimum(jnp.abs(res[0:1, 0:LANES]), 0.0)

            cs = pl.ds(pl.multiple_of(c * LANES, LANES), LANES)
            n_groups = (PEER_KEYS // PEER_JH) * (PEER_TE // PEER_KEYS // PEER_IU)
            group = 0
            anchor = None
            for jh in range(PEER_KEYS // PEER_JH):
                js = slice(jh * PEER_JH, (jh + 1) * PEER_JH)
                for ig in range(PEER_TE // PEER_KEYS // PEER_IU):
                    if group % (n_groups // 4) == 0:
                        anchor = mxu_piece(group // (n_groups // 4))
                    group += 1
                    w = [jnp.zeros((PEER_JH, LANES), F32) + anchor for _ in range(PEER_IU)]
                    for h in range(PEER_HEADS):
                        s1 = s1_ref[h, c, js, :]
                        e1 = e1_ref[h, c, js, :]
                        for t in range(PEER_IU):
                            ii = ig * PEER_IU + t
                            sel = s1 >= t0_ref[h, c, ii:ii + 1, :]
                            w[t] = w[t] + jnp.where(sel, e1 * e0_ref[h, c, ii:ii + 1, :], 0.0)
                    for t in range(PEER_IU):
                        r0 = (ig * PEER_IU + t) * PEER_KEYS + jh * PEER_JH
                        rows = slice(r0, r0 + PEER_JH)
                        a_w[rows, cs] = (jax.nn.gelu(st_r[rows, cs]) * w[t]).astype(BF16)
            return carry

        lax.fori_loop(0, ncol, body, 0)

    @pl.when(j % 2 == 0)
    def _():
        run(st_a, st_b, a_b, a_a)

    @pl.when(j % 2 == 1)
    def _():
        run(st_b, st_a, a_a, a_b)

    @pl.when(j == pl.num_programs(1) - 1)
    def _():
        out = h_ref[...] + m_ref[...] * acc_scr[...].T
        if final_norm:
            out = out * lax.rsqrt(jnp.mean(out * out, axis=-1, keepdims=True) + RMS_EPS) * fg_ref[...]
        o_ref[...] = out


def peer_dense(x_bf, u_bf, vt_bf, t0, e0, s1, e1, h, gate, final_g):
    T, D = h.shape
    tq = min(T, 512)
    ncol = tq // LANES
    ni = PEER_TE // PEER_KEYS
    n_tiles = PEER_EXPERTS // PEER_TE
    final_norm = final_g is not None
    fg = (final_g if final_norm else jnp.ones((D,), F32)).reshape(1, D)
    tile = lambda j, lag: jnp.clip(j - lag, 0, n_tiles - 1)
    tok_rows = pl.BlockSpec((tq, D), lambda i, j: (i, 0))
    head_all = pl.BlockSpec((PEER_HEADS, ncol, PEER_KEYS, LANES), lambda i, j: (0, i, 0, 0))
    head_blk = pl.BlockSpec((PEER_HEADS, ncol, ni, LANES), lambda i, j: (0, i, tile(j, 1), 0))
    return pl.pallas_call(
        functools.partial(_peer_dense_kernel, final_norm=final_norm),
        grid=(T // tq, n_tiles + 2),
        in_specs=[tok_rows,
                  pl.BlockSpec((PEER_TE, D), lambda i, j: (tile(j, 0), 0)),
                  pl.BlockSpec((D, PEER_TE), lambda i, j: (0, tile(j, 2))),
                  head_blk, head_blk, head_all, head_all,
                  tok_rows,
                  pl.BlockSpec((1, D), lambda i, j: (0, 0)),
                  pl.BlockSpec((1, D), lambda i, j: (0, 0))],
        out_specs=pl.BlockSpec((tq, D), lambda i, j: (i, 0)),
        out_shape=jax.ShapeDtypeStruct((T, D), F32),
        scratch_shapes=[pltpu.VMEM((D, tq), F32),
                        pltpu.VMEM((PEER_TE, tq), F32), pltpu.VMEM((PEER_TE, tq), F32),
                        pltpu.VMEM((PEER_TE, tq), BF16), pltpu.VMEM((PEER_TE, tq), BF16)],
        compiler_params=_cparams("parallel", "arbitrary"),
        name="peer_dense",
    )(x_bf, u_bf, vt_bf, t0, e0, s1, e1, h, gate.reshape(1, D), fg)


def peer_block(h, norm_g, shift, scale, gate, wq_bf, subkeys, u_bf, vt_bf, final_g=None):
    q, x_bf = nmm(h, norm_g, shift, scale, wq_bf, tn=1024)
    t0, e0, s1, e1 = peer_topk(q, subkeys)
    return peer_dense(x_bf, u_bf, vt_bf, t0, e0, s1, e1, h, gate, final_g)


def _softplus(x):
    return jnp.maximum(x, 0.0) + jnp.log(1.0 + jnp.exp(-jnp.abs(x)))


def _gdn_prep_kernel(p_ref, cw_ref, alog_ref, dtb_ref, q_ref, k_ref, v_ref, g_ref, gt_ref, *, width):
    tm = p_ref.shape[0]
    pos = lax.broadcasted_iota(jnp.int32, (tm, 1), 0) % width
    first = pos == 0
    last = pos == width - 1
    for cb in range(DN_CONV_CH // LANES):
        cs = slice(cb * LANES, (cb + 1) * LANES)
        x = p_ref[:, cs]
        prev = jnp.where(first, 0.0, pltpu.roll(x, 1, 0))
        nxt = jnp.where(last, 0.0, pltpu.roll(x, tm - 1, 0))
        c = _silu(prev * cw_ref[0:1, cs] + x * cw_ref[1:2, cs] + nxt * cw_ref[2:3, cs])
        hs = slice((cb % DN_HEADS) * LANES, (cb % DN_HEADS + 1) * LANES)
        if cb < DN_HEADS:
            q_ref[:, hs] = c * (lax.rsqrt(jnp.sum(c * c, axis=-1, keepdims=True) + 1e-6) * DN_DK ** -0.5)
        elif cb < 2 * DN_HEADS:
            k_ref[:, hs] = c * lax.rsqrt(jnp.sum(c * c, axis=-1, keepdims=True) + 1e-6)
        else:
            v_ref[:, hs] = c
    ab = p_ref[:, DN_IN_PAD - LANES:DN_IN_PAD]
    lane = lax.broadcasted_iota(jnp.int32, (1, LANES), 1)
    g = -jnp.exp(alog_ref[...]) * _softplus(ab + dtb_ref[...])
    gates = jnp.where(lane < 2 * DN_HEADS, g, jnp.where(lane < 4 * DN_HEADS, jax.nn.sigmoid(ab), 0.0))
    row = lax.broadcasted_iota(jnp.int32, (DN_CHUNK, DN_CHUNK), 0)
    col = lax.broadcasted_iota(jnp.int32, (DN_CHUNK, DN_CHUNK), 1)
    lower = (col <= row).astype(F32)
    upper = (col >= row).astype(F32)
    for ch in range(tm // DN_CHUNK):
        rs = slice(ch * DN_CHUNK, (ch + 1) * DN_CHUNK)
        pre = jnp.dot(lower, gates[rs], precision=HI, preferred_element_type=F32)
        suf = jnp.dot(upper, gates[rs], precision=HI, preferred_element_type=F32)
        g_ref[rs, :] = jnp.where(lane < DN_HEADS, pre, jnp.where(lane < 2 * DN_HEADS, suf, gates[rs]))
    gt_ref[...] = g_ref[...].T


def gdn_prep(p, conv_w, a_log, dt_bias, width):
    T = p.shape[0]
    tm = 256
    pad = lambda v: jnp.zeros((1, LANES), F32).at[0, :2 * DN_HEADS].set(v.reshape(-1))
    qkv = jax.ShapeDtypeStruct((T, DN_HEADS * DN_DK), F32)
    rows = lambda n: pl.BlockSpec((tm, n), lambda i: (i, 0))
    q, k, v, g, g_t = pl.pallas_call(
        functools.partial(_gdn_prep_kernel, width=width),
        grid=(T // tm,),
        in_specs=[rows(DN_IN_PAD),
                  pl.BlockSpec((3, DN_CONV_CH), lambda i: (0, 0)),
                  pl.BlockSpec((1, LANES), lambda i: (0, 0)),
                  pl.BlockSpec((1, LANES), lambda i: (0, 0))],
        out_specs=[rows(DN_HEADS * DN_DK)] * 3 + [rows(LANES), pl.BlockSpec((LANES, tm), lambda i: (0, i))],
        out_shape=[qkv, qkv, qkv, jax.ShapeDtypeStruct((T, LANES), F32),
                   jax.ShapeDtypeStruct((LANES, T), F32)],
        compiler_params=_cparams("parallel"),
        name="gdn_prep",
    )(p, conv_w, pad(a_log), pad(dt_bias))
    g_row = g_t[:2 * DN_HEADS].reshape(2 * DN_HEADS, T // DN_CHUNK, DN_CHUNK).transpose(1, 0, 2)
    return q, k, v, g, g_row


def _gdn_chunk_kernel(qf_ref, kf_ref, vf_ref, gf_ref, rf_ref, qb_ref, kb_ref, vb_ref, gb_ref, rb_ref,
                      s0_ref, of_ref, ob_ref, sfin_ref, s_scr):
    C = DN_CHUNK

    @pl.when(pl.program_id(0) == 0)
    def _():
        s_scr[...] = s0_ref[...]

    row = lax.broadcasted_iota(jnp.int32, (C, C), 0)
    col = lax.broadcasted_iota(jnp.int32, (C, C), 1)
    eye = (row == col).astype(F32)
    streams = ((qf_ref, kf_ref, vf_ref, gf_ref, rf_ref, of_ref), (qb_ref, kb_ref, vb_ref, gb_ref, rb_ref, ob_ref))
    chains = [(d, h) for d in range(2) for h in range(DN_HEADS)]
    mm = functools.partial(jnp.dot, preferred_element_type=F32)
    mm_nt = functools.partial(lax.dot_general, dimension_numbers=NT_DIMS, preferred_element_type=F32)
    mm_tn = functools.partial(lax.dot_general, dimension_numbers=TN_DIMS, preferred_element_type=F32)

    st = []
    for d, h in chains:
        q_ref, k_ref, v_ref, g_ref, r_ref, _ = streams[d]
        hs = slice(h * DN_DK, (h + 1) * DN_DK)
        gl = DN_HEADS * d + h
        inc = (col <= row) if d == 0 else (col >= row)
        strict = (col < row) if d == 0 else (col > row)
        gc = jnp.broadcast_to(g_ref[:, gl:gl + 1], (C, LANES))
        beta_b = jnp.broadcast_to(g_ref[:, 2 * DN_HEADS + gl:2 * DN_HEADS + gl + 1], (C, LANES))
        decay = jnp.where(inc, jnp.exp(jnp.where(inc, gc[:, :C] - r_ref[0, gl:gl + 1, :], 0.0)), 0.0)
        g_tot = gc[C - 1:C, :] if d == 0 else gc[0:1, :]
        k = k_ref[:, hs]
        k_beta = k * beta_b
        k_bf = k.astype(BF16)
        a = jnp.where(strict, mm_nt(k_beta.astype(BF16), k_bf) * decay, 0.0)
        e_gc = jnp.exp(gc)
        rhs = jnp.concatenate([v_ref[:, hs] * beta_b, k_beta * e_gc], axis=1).astype(BF16)
        q = q_ref[:, hs]
        attn = (mm_nt(q.astype(BF16), k_bf) * decay).astype(BF16)
        st.append(dict(hs=hs, a=a, rhs=rhs, attn=attn, qd=q * e_gc, g_tot=g_tot,
                       k_dec=(k * jnp.exp(g_tot - gc)).astype(BF16), t_inv=eye - a, p=a))
    for _ in range(5):
        for c in st:
            p_bf = c['p'].astype(BF16)
            c['p'] = mm(p_bf, p_bf)
        for c in st:
            c['t_inv'] = c['t_inv'] + mm(c['t_inv'].astype(BF16), c['p'].astype(BF16))
    for c in st:
        c['uw'] = mm(c['t_inv'].astype(BF16), c['rhs'])
    for c, (d, h) in zip(st, chains):
        c['s'] = s_scr[d, h]
        c['ws'] = mm(jnp.concatenate([c['uw'][:, DN_DK:], c['qd']], axis=0).astype(BF16), c['s'].astype(BF16))
    for c in st:
        c['v_new'] = (c['uw'][:, :DN_DK] - c['ws'][:C]).astype(BF16)
    for c, (d, h) in zip(st, chains):
        streams[d][5][:, c['hs']] = c['ws'][C:] + mm(c['attn'], c['v_new'])
    for c, (d, h) in zip(st, chains):
        s_scr[d, h] = c['s'] * jnp.exp(c['g_tot']) + mm_tn(c['k_dec'], c['v_new'])
    sfin_ref[...] = s_scr[...]


def gdn_chunk(q, k, v, g, g_row, s0):
    T, HD = q.shape
    n = T // DN_CHUNK
    fwd = lambda i: (i, 0)
    bwd = lambda i: (n - 1 - i, 0)
    qkv = lambda m: pl.BlockSpec((DN_CHUNK, HD), m)
    gsp = lambda m: pl.BlockSpec((DN_CHUNK, LANES), m)
    rsp = lambda m: pl.BlockSpec((1, 2 * DN_HEADS, DN_CHUNK), lambda i: m(i) + (0,))
    sspec = pl.BlockSpec((2, DN_HEADS, DN_DK, DN_DK), lambda i: (0, 0, 0, 0))
    return pl.pallas_call(
        _gdn_chunk_kernel,
        grid=(n,),
        in_specs=[qkv(fwd), qkv(fwd), qkv(fwd), gsp(fwd), rsp(fwd),
                  qkv(bwd), qkv(bwd), qkv(bwd), gsp(bwd), rsp(bwd), sspec],
        out_specs=[qkv(fwd), qkv(bwd), sspec],
        out_shape=[jax.ShapeDtypeStruct((T, HD), F32), jax.ShapeDtypeStruct((T, HD), F32),
                   jax.ShapeDtypeStruct((2, DN_HEADS, DN_DK, DN_DK), F32)],
        scratch_shapes=[pltpu.VMEM((2, DN_HEADS, DN_DK, DN_DK), F32)],
        compiler_params=_cparams("arbitrary"),
        name="gdn_chunk",
    )(q, k, v, g, g_row, q, k, v, g, g_row, s0)


def _gdn_out_kernel(of_ref, ob_ref, z_ref, ng_ref, w_ref, h_ref, m_ref, o_ref):
    ys = []
    for h in range(DN_HEADS):
        hs = slice(h * DN_DK, (h + 1) * DN_DK)
        o = of_ref[:, hs] + ob_ref[:, hs]
        y = o * lax.rsqrt(jnp.mean(o * o, axis=-1, keepdims=True) + RMS_EPS) * ng_ref[...]
        ys.append((y * _silu(z_ref[:, hs])).astype(BF16))
    out = jnp.dot(jnp.concatenate(ys, axis=1), w_ref[...], preferred_element_type=F32)
    o_ref[...] = h_ref[...] + m_ref[...] * out


def gdn_out(o_f, o_b, p, norm_g, w_out_bf, h, gate):
    T, D = h.shape
    tm = min(T, 512)
    rows = lambda n: pl.BlockSpec((tm, n), lambda i: (i, 0))
    return pl.pallas_call(
        _gdn_out_kernel,
        grid=(T // tm,),
        in_specs=[rows(D), rows(D),
                  pl.BlockSpec((tm, D), lambda i: (i, DN_CONV_CH // D)),
                  pl.BlockSpec((1, DN_DK), lambda i: (0, 0)),
                  pl.BlockSpec((D, D), lambda i: (0, 0)),
                  rows(D),
                  pl.BlockSpec((1, D), lambda i: (0, 0))],
        out_specs=rows(D),
        out_shape=jax.ShapeDtypeStruct((T, D), F32),
        compiler_params=_cparams("parallel"),
        name="gdn_out",
    )(o_f, o_b, p, norm_g.reshape(1, -1), w_out_bf, h, gate.reshape(1, D))


def kernel(x, c, ctx, c_ctx, ada_w, ada_b, norm1_g, norm2_g, e_w_in, e_conv_w, s5_lam_re, s5_lam_im,
           s5_log_dt, s5_b_re, s5_b_im, s5_c_re, s5_c_im, s5_d, s5_glu_w, s5_glu_b, e_w_out, o_w_in,
           dn_conv_w, dn_a_log, dn_dt_bias, dn_norm_g, o_w_out, peer_wq, peer_subkeys, peer_u, peer_v,
           final_g):
    D = D_MODEL
    h_lat = x[0]
    h_ctx = ctx[0]
    ctx_len = h_ctx.shape[0]
    cv = jnp.zeros((SUBLANES, D), F32).at[0].set(c[0]).at[1].set(c_ctx)
    mods = adaln(cv, ada_w, ada_b)
    depth = ada_w.shape[0]
    for l in range(depth):
        last = l == depth - 1
        i = l // 2
        m_lat = mods[l, 0].reshape(N_MOD, D)
        m_ctx = mods[l, 1].reshape(N_MOD, D)
        if l % 2 == 0:
            w_in = e_w_in[i].astype(BF16)
            p_lat, _ = nmm(h_lat, norm1_g[l], m_lat[0], m_lat[1], w_in, tn=1024)
            p_ctx, _ = nmm(h_ctx, norm1_g[l], m_ctx[0], m_ctx[1], w_in, tn=1024)
            u_lat = _to_segments(p_lat[:, 3 * CONV_CH:])
            u_ctx = _to_segments(p_ctx[:, 3 * CONV_CH:])
            y_lat, y_ctx = [], []
            zero = (jnp.zeros((SSM_STATE,), F32), jnp.zeros((SSM_STATE,), F32))
            for di in range(2):
                bd, cd, ab_re, ab_im = s5_params(s5_lam_re[i, di], s5_lam_im[i, di], s5_log_dt[i, di],
                                                 s5_b_re[i, di], s5_b_im[i, di], s5_c_re[i, di],
                                                 s5_c_im[i, di])
                yc, fin = s5_direction(u_ctx, bd, cd, ab_re, ab_im, zero, di == 1)
                yl, _ = s5_direction(u_lat, bd, cd, ab_re, ab_im, fin, di == 1)
                y_ctx.append(_from_segments(yc))
                y_lat.append(_from_segments(yl))
            glu_w = s5_glu_w[i].astype(BF16)
            w_out = e_w_out[i].astype(BF16)
            h_lat = even_post(p_lat, y_lat[0], y_lat[1], h_lat, e_conv_w[i], s5_d[i], glu_w, s5_glu_b[i],
                              w_out, m_lat[2], GRID_W)
            if not last:
                h_ctx = even_post(p_ctx, y_ctx[0], y_ctx[1], h_ctx, e_conv_w[i], s5_d[i], glu_w,
                                  s5_glu_b[i], w_out, m_ctx[2], ctx_len)
        else:
            w_in = jnp.pad(o_w_in[i], ((0, 0), (0, DN_IN_PAD - o_w_in.shape[2]))).astype(BF16)
            p_lat, _ = nmm(h_lat, norm1_g[l], m_lat[0], m_lat[1], w_in, tn=DN_IN_PAD // 3)
            p_ctx, _ = nmm(h_ctx, norm1_g[l], m_ctx[0], m_ctx[1], w_in, tn=DN_IN_PAD // 3)
            qc, kc, vc, gc, rc = gdn_prep(p_ctx, dn_conv_w[i], dn_a_log[i], dn_dt_bias[i], ctx_len)
            ql, kl, vl, gl, rl = gdn_prep(p_lat, dn_conv_w[i], dn_a_log[i], dn_dt_bias[i], GRID_W)
            zero = jnp.zeros((2, DN_HEADS, DN_DK, DN_DK), F32)
            oc_f, oc_b, s_ctx = gdn_chunk(qc, kc, vc, gc, rc, zero)
            ol_f, ol_b, _ = gdn_chunk(ql, kl, vl, gl, rl, s_ctx)
            w_out = o_w_out[i].astype(BF16)
            h_lat = gdn_out(ol_f, ol_b, p_lat, dn_norm_g[i], w_out, h_lat, m_lat[2])
            if not last:
                h_ctx = gdn_out(oc_f, oc_b, p_ctx, dn_norm_g[i], w_out, h_ctx, m_ctx[2])
        wq = peer_wq[l].astype(BF16)
        u_bf = peer_u[l].astype(BF16)
        vt_bf = peer_v[l].T.astype(BF16)
        h_lat = peer_block(h_lat, norm2_g[l], m_lat[3], m_lat[4], m_lat[5], wq, peer_subkeys[l], u_bf,
                           vt_bf, final_g if last else None)
        if not last:
            h_ctx = peer_block(h_ctx, norm2_g[l], m_ctx[3], m_ctx[4], m_ctx[5], wq, peer_subkeys[l], u_bf,
                               vt_bf)
    return h_lat[None]
```

```python
import functools
import math

import jax
import jax.numpy as jnp
from jax import lax
from jax.experimental import pallas as pl
from jax.experimental.pallas import tpu as pltpu

F32 = jnp.float32
BF16 = jnp.bfloat16
HI = lax.Precision.HIGHEST

D_MODEL = 1024
GRID_W = 64
RMS_EPS = 1e-6
N_MOD = 6
CONV_CH = 512
SSM_CH = 512
SSM_H = 16
SSM_G = 32
SSM_P = 64
SSM_STATE = SSM_G * SSM_P
DN_HEADS = 8
DN_DK = 128
DN_CHUNK = 64
DN_CONV_CH = 3072
DN_IN_PAD = 4224
PEER_HEADS = 8
PEER_KEYS = 128
PEER_TOPK = 16
PEER_EXPERTS = PEER_KEYS * PEER_KEYS

SUBLANES = 8
LANES = 128
VMEM_LIMIT = 52 * 1024 * 1024

NT_DIMS = (((1,), (1,)), ((), ()))
TN_DIMS = (((0,), (0,)), ((), ()))


def _cparams(*sem):
    return pltpu.CompilerParams(dimension_semantics=sem, vmem_limit_bytes=VMEM_LIMIT)


def _silu(x):
    return x * jax.nn.sigmoid(x)


def _adaln_kernel(c_ref, w_ref, b_ref, o_ref):
    sc = _silu(c_ref[...])
    o_ref[0] = jnp.dot(sc, w_ref[0], precision=HI, preferred_element_type=F32) + b_ref[0]


def adaln(cv, ada_w, ada_b):
    L, D, N = ada_w.shape
    tn = 1536
    return pl.pallas_call(
        _adaln_kernel,
        grid=(L, N // tn),
        in_specs=[pl.BlockSpec((SUBLANES, D), lambda l, j: (0, 0)),
                  pl.BlockSpec((1, D, tn), lambda l, j: (l, 0, j)),
                  pl.BlockSpec((1, 1, tn), lambda l, j: (l, 0, j))],
        out_specs=pl.BlockSpec((1, SUBLANES, tn), lambda l, j: (l, 0, j)),
        out_shape=jax.ShapeDtypeStruct((L, SUBLANES, N), F32),
        compiler_params=_cparams("parallel", "parallel"),
        name="adaln",
    )(cv, ada_w, ada_b.reshape(L, 1, N))


def _nmm_kernel(x_ref, g_ref, sh_ref, sc_ref, w_ref, o_ref, a_ref):
    x = x_ref[...]
    y = x * lax.rsqrt(jnp.mean(x * x, axis=-1, keepdims=True) + RMS_EPS)
    a = ((y * g_ref[...]) * (1.0 + sc_ref[...]) + sh_ref[...]).astype(BF16)
    a_ref[...] = a
    o_ref[...] = jnp.dot(a, w_ref[...], preferred_element_type=F32)


def nmm(x, g, shift, scale, w_bf):
    T, D = x.shape
    N = w_bf.shape[1]
    tm = min(T, 512)
    row = lambda v: v.reshape(1, D)
    return pl.pallas_call(
        _nmm_kernel,
        grid=(T // tm,),
        in_specs=[pl.BlockSpec((tm, D), lambda i: (i, 0)),
                  pl.BlockSpec((1, D), lambda i: (0, 0)),
                  pl.BlockSpec((1, D), lambda i: (0, 0)),
                  pl.BlockSpec((1, D), lambda i: (0, 0)),
                  pl.BlockSpec((D, N), lambda i: (0, 0))],
        out_specs=[pl.BlockSpec((tm, N), lambda i: (i, 0)),
                   pl.BlockSpec((tm, D), lambda i: (i, 0))],
        out_shape=[jax.ShapeDtypeStruct((T, N), F32),
                   jax.ShapeDtypeStruct((T, D), BF16)],
        compiler_params=_cparams("parallel"),
        name="nmm",
    )(x, row(g), row(shift), row(scale), w_bf)


S5_COLS = 512


def _s5_kernel(u_ref, bd_ref, cd_ref, a_ref, init_ref, y_ref, ends_ref, s_scr, st_scr,
               *, reverse, tj, emit_y):
    @pl.when(pl.program_id(0) == 0)
    def _():
        st_scr[...] = init_ref[...]

    s_scr[...] = jnp.dot(u_ref[...].astype(BF16), bd_ref[...], preferred_element_type=F32)
    for cc in range(SSM_STATE // S5_COLS):
        re = slice(cc * S5_COLS, (cc + 1) * S5_COLS)
        im = slice(SSM_STATE + cc * S5_COLS, SSM_STATE + (cc + 1) * S5_COLS)
        ar = a_ref[:, re]
        ai = a_ref[:, im]

        def body(jj, carry, re=re, im=im, ar=ar, ai=ai):
            sr, si = carry
            j = (tj - 1 - jj) if reverse else jj
            rows = pl.ds(pl.multiple_of(j * SUBLANES, SUBLANES), SUBLANES)
            nr = ar * sr - ai * si + s_scr[rows, re]
            ni = ar * si + ai * sr + s_scr[rows, im]
            s_scr[rows, re] = nr
            s_scr[rows, im] = ni
            return nr, ni

        sr, si = lax.fori_loop(0, tj, body, (st_scr[:, re], st_scr[:, im]))
        st_scr[:, re] = sr
        st_scr[:, im] = si
    if emit_y:
        y_ref[...] = jnp.dot(s_scr[...].astype(BF16), cd_ref[...], preferred_element_type=F32)
    else:
        y_ref[...] = jnp.zeros_like(y_ref)
    ends_ref[...] = st_scr[...]


def s5_pass(u_perm, bd, cd, a_t, init, reverse, emit_y):
    T, C = u_perm.shape
    tt = min(T, 512)
    nt = T // tt
    idx = (lambda i: (nt - 1 - i, 0)) if reverse else (lambda i: (i, 0))
    kern = functools.partial(_s5_kernel, reverse=reverse, tj=tt // SUBLANES, emit_y=emit_y)
    yt = tt if emit_y else SUBLANES
    return pl.pallas_call(
        kern,
        grid=(nt,),
        in_specs=[pl.BlockSpec((tt, C), idx),
                  pl.BlockSpec((C, 2 * SSM_STATE), lambda i: (0, 0)),
                  pl.BlockSpec((2 * SSM_STATE, C), lambda i: (0, 0)),
                  pl.BlockSpec((SUBLANES, 2 * SSM_STATE), lambda i: (0, 0)),
                  pl.BlockSpec((SUBLANES, 2 * SSM_STATE), lambda i: (0, 0))],
        out_specs=[pl.BlockSpec((yt, C), idx if emit_y else (lambda i: (0, 0))),
                   pl.BlockSpec((SUBLANES, 2 * SSM_STATE), lambda i: (0, 0))],
        out_shape=[jax.ShapeDtypeStruct((T if emit_y else SUBLANES, C), F32),
                   jax.ShapeDtypeStruct((SUBLANES, 2 * SSM_STATE), F32)],
        scratch_shapes=[pltpu.VMEM((tt, 2 * SSM_STATE), F32),
                        pltpu.VMEM((SUBLANES, 2 * SSM_STATE), F32)],
        compiler_params=_cparams("arbitrary"),
        name="s5_scan",
    )(u_perm, bd, cd, a_t, init)


def _cmul(ar, ai, br, bi):
    return ar * br - ai * bi, ar * bi + ai * br


def s5_direction(u_perm, bd, cd, ab_re, ab_im, s0, reverse):
    T = u_perm.shape[0]
    tseg = T // SUBLANES
    a_t = jnp.broadcast_to(jnp.concatenate([ab_re, ab_im])[None, :], (SUBLANES, 2 * SSM_STATE))
    _, ends = s5_pass(u_perm, bd, cd, a_t, jnp.zeros((SUBLANES, 2 * SSM_STATE), F32), reverse, False)
    pr, pi = ab_re, ab_im
    for _ in range(int(math.log2(tseg))):
        pr, pi = _cmul(pr, pi, pr, pi)
    er, ei = ends[:, :SSM_STATE], ends[:, SSM_STATE:]
    cr, ci = s0
    starts = [None] * SUBLANES
    order = range(SUBLANES - 1, -1, -1) if reverse else range(SUBLANES)
    for r in order:
        starts[r] = jnp.concatenate([cr, ci])
        nr, ni = _cmul(pr, pi, cr, ci)
        cr, ci = nr + er[r], ni + ei[r]
    y, _ = s5_pass(u_perm, bd, cd, a_t, jnp.stack(starts), reverse, True)
    return y, (cr, ci)


def s5_params(lam_re, lam_im, log_dt, b_re, b_im, c_re, c_im):
    lam_re = jnp.minimum(lam_re, -1e-4)
    dt = jnp.exp(log_dt)[:, None]
    mag = jnp.exp(lam_re * dt)
    ab_re = mag * jnp.cos(lam_im * dt)
    ab_im = mag * jnp.sin(lam_im * dt)
    den = lam_re * lam_re + lam_im * lam_im
    f_re = ((ab_re - 1) * lam_re + ab_im * lam_im) / den
    f_im = (ab_im * lam_re - (ab_re - 1) * lam_im) / den
    bb_re = f_re[..., None] * b_re - f_im[..., None] * b_im
    bb_im = f_re[..., None] * b_im + f_im[..., None] * b_re
    eye = jnp.eye(SSM_G, dtype=F32)
    blk_in = lambda t: jnp.einsum('gph,gk->ghkp', t, eye).reshape(SSM_CH, SSM_STATE)
    blk_out = lambda t: jnp.einsum('ghp,gk->gpkh', t, eye).reshape(SSM_STATE, SSM_CH)
    bd = jnp.concatenate([blk_in(bb_re), blk_in(bb_im)], axis=1).astype(BF16)
    cd = jnp.concatenate([blk_out(c_re), -blk_out(c_im)], axis=0).astype(BF16)
    return bd, cd, ab_re.reshape(-1), ab_im.reshape(-1)


def _to_segments(u):
    T, C = u.shape
    return u.reshape(SUBLANES, T // SUBLANES, C).transpose(1, 0, 2).reshape(T, C)


def _from_segments(y):
    T, C = y.shape
    return y.reshape(T // SUBLANES, SUBLANES, C).transpose(1, 0, 2).reshape(T, C)


def _conv3(x, w_ref, width):
    tm = x.shape[0]
    pos = lax.broadcasted_iota(jnp.int32, (tm, 1), 0) % width
    prev = jnp.where(pos == 0, 0.0, pltpu.roll(x, 1, 0))
    nxt = jnp.where(pos == width - 1, 0.0, pltpu.roll(x, tm - 1, 0))
    return prev * w_ref[0:1, :] + x * w_ref[1:2, :] + nxt * w_ref[2:3, :]


def _even_post_kernel(p_ref, yf_ref, yb_ref, h_ref, cw_ref, d_ref, gw_ref, gb_ref, wo_ref, m_ref,
                      o_ref, *, width):
    u = p_ref[:, 0:CONV_CH]
    g_in = p_ref[:, CONV_CH:2 * CONV_CH]
    g_out = p_ref[:, 2 * CONV_CH:3 * CONV_CH]
    branch_a = g_out * _conv3(g_in * u, cw_ref, width)
    s = p_ref[:, 3 * CONV_CH:] * d_ref[...] + yf_ref[...] + yb_ref[...]
    y = jax.nn.gelu(s)
    gate = jnp.dot(y.astype(BF16), gw_ref[...], preferred_element_type=F32) + gb_ref[...]
    branch_b = y * jax.nn.sigmoid(gate)
    out = (jnp.dot(branch_a.astype(BF16), wo_ref[0:CONV_CH, :], preferred_element_type=F32)
           + jnp.dot(branch_b.astype(BF16), wo_ref[CONV_CH:, :], preferred_element_type=F32))
    o_ref[...] = h_ref[...] + m_ref[...] * out


def even_post(p, yf, yb, h, conv_w, d_skip, glu_w_bf, glu_b, w_out_bf, gate, width):
    T, D = h.shape
    tm = min(T, 512)
    full = lambda shape: pl.BlockSpec(shape, lambda i: (0, 0))
    rows = lambda n: pl.BlockSpec((tm, n), lambda i: (i, 0))
    return pl.pallas_call(
        functools.partial(_even_post_kernel, width=width),
        grid=(T // tm,),
        in_specs=[rows(4 * CONV_CH), rows(SSM_CH), rows(SSM_CH), rows(D),
                  full((3, CONV_CH)), full((1, SSM_CH)), full((SSM_CH, SSM_CH)), full((1, SSM_CH)),
                  full((2 * CONV_CH, D)), full((1, D))],
        out_specs=rows(D),
        out_shape=jax.ShapeDtypeStruct((T, D), F32),
        compiler_params=_cparams("parallel"),
        name="even_post",
    )(p, yf, yb, h, conv_w, d_skip.reshape(1, -1), glu_w_bf, glu_b.reshape(1, -1), w_out_bf,
      gate.reshape(1, D))


def _extract_top(w, n, want_rank=False):
    vals = []
    rank = jnp.full(w.shape, float(n), F32) if want_rank else None
    for k in range(n):
        m = jnp.max(w, axis=0, keepdims=True)
        vals.append(m)
        hit = w == m
        if want_rank:
            rank = jnp.where(hit, float(k), rank)
        w = jnp.where(hit, -jnp.inf, w)
    return vals, rank


_PEER_PAIRS = [(k, l) for k in range(PEER_TOPK + 1) for l in range(PEER_TOPK + 1)
               if (k + 1) * (l + 1) <= PEER_TOPK + 1]
_PEER_CAND_ROWS = -(-len(_PEER_PAIRS) // SUBLANES) * SUBLANES


def _peer_topk_kernel(q_ref, sk_ref, c0_ref, e0_ref, r1_ref, e1_ref, cand_scr):
    tq = q_ref.shape[0]
    cand_scr[...] = jnp.full(cand_scr.shape, -jnp.inf, F32)
    for h in range(PEER_HEADS):
        sc = []
        for s in range(2):
            col = (2 * h + s) * LANES
            sc.append(lax.dot_general(sk_ref[s], q_ref[:, col:col + LANES], NT_DIMS,
                                      precision=HI, preferred_element_type=F32))
        top0, _ = _extract_top(sc[0], PEER_TOPK + 1)
        top1, rank1 = _extract_top(sc[1], PEER_TOPK + 1, want_rank=True)
        for r, (k, l) in enumerate(_PEER_PAIRS):
            cand_scr[r:r + 1, :] = top0[k] + top1[l]
        best, _ = _extract_top(cand_scr[...], PEER_TOPK + 1)
        z = jnp.zeros_like(best[0])
        for k in range(PEER_TOPK):
            z = z + jnp.exp(best[k] - best[0])
        t0 = 0.5 * (best[PEER_TOPK - 1] + best[PEER_TOPK]) - sc[0]
        count = jnp.zeros_like(t0)
        for k in range(PEER_TOPK):
            count = count + jnp.where(top1[k] >= t0, 1.0, 0.0)
        e0 = jnp.exp(sc[0] - top0[0]) / z
        rank1 = pltpu.bitcast(rank1.astype(BF16), jnp.uint32)
        e1 = pltpu.bitcast(jnp.exp(sc[1] - top1[0]).astype(BF16), jnp.uint32)
        for c in range(tq // LANES):
            cs = slice(c * LANES, (c + 1) * LANES)
            c0_ref[h, c] = count[:, cs]
            e0_ref[h, c] = e0[:, cs]
            r1_ref[h, c] = rank1[:, cs]
            e1_ref[h, c] = e1[:, cs]


def peer_topk(q, subkeys):
    T = q.shape[0]
    tq = min(T, 256)
    table = lambda rows, dt: jax.ShapeDtypeStruct((PEER_HEADS, T // LANES, rows, LANES), dt)
    bspec = lambda rows: pl.BlockSpec((PEER_HEADS, tq // LANES, rows, LANES), lambda i: (0, i, 0, 0))
    return pl.pallas_call(
        _peer_topk_kernel,
        grid=(T // tq,),
        in_specs=[pl.BlockSpec((tq, q.shape[1]), lambda i: (i, 0)),
                  pl.BlockSpec((2, PEER_KEYS, LANES), lambda i: (0, 0, 0))],
        out_specs=[bspec(PEER_KEYS), bspec(PEER_KEYS), bspec(PEER_KEYS // 2), bspec(PEER_KEYS // 2)],
        out_shape=[table(PEER_KEYS, F32), table(PEER_KEYS, F32),
                   table(PEER_KEYS // 2, jnp.uint32), table(PEER_KEYS // 2, jnp.uint32)],
        scratch_shapes=[pltpu.VMEM((_PEER_CAND_ROWS, tq), F32)],
        compiler_params=_cparams("parallel"),
        name="peer_topk",
    )(q, subkeys)


PEER_TE = 1024
PEER_JH = 16
PEER_IU = 2


def _peer_dense_kernel(x_ref, u_ref, vt_ref, c0_ref, e0_ref, r1_ref, e1_ref, h_ref, m_ref, fg_ref, o_ref,
                       acc_scr, st_a, st_b, a_a, a_b, *, final_norm):
    j = pl.program_id(1)
    tq, d = x_ref.shape
    ncol = tq // LANES
    e_rows = PEER_TE // ncol
    d_rows = d // ncol

    def packed_row(row):
        return jnp.broadcast_to(row, (PEER_JH, LANES)).astype(BF16)

    @pl.when(j == 0)
    def _():
        acc_scr[...] = jnp.zeros_like(acc_scr)
        st_a[...] = jnp.zeros_like(st_a)
        st_b[...] = jnp.zeros_like(st_b)
        a_a[...] = jnp.zeros_like(a_a)
        a_b[...] = jnp.zeros_like(a_b)

    def run(st_w, st_r, a_w, a_r):
        def body(c, carry):
            er = pl.ds(pl.multiple_of(c * e_rows, e_rows), e_rows)
            dr = pl.ds(pl.multiple_of(c * d_rows, d_rows), d_rows)
            half = tq // 2

            def mxu_piece(k):
                ts = slice((k % 2) * half, (k % 2 + 1) * half)
                if k < 2:
                    res = lax.dot_general(u_ref[er, :], x_ref[ts, :], NT_DIMS, preferred_element_type=F32)
                    st_w[er, ts] = res
                else:
                    res = acc_scr[dr, ts] + jnp.dot(vt_ref[dr, :], a_r[:, ts], preferred_element_type=F32)
                    acc_scr[dr, ts] = res
                return jnp.minimum(jnp.abs(res[0:1, 0:LANES]), 0.0)

            cs = pl.ds(pl.multiple_of(c * LANES, LANES), LANES)
            n_slabs = PEER_KEYS // PEER_JH
            for ig in range(PEER_TE // PEER_KEYS // PEER_IU):
                anchor = mxu_piece(ig)
                w = [[jnp.zeros((PEER_JH, LANES), BF16) for _ in range(n_slabs)] for _ in range(PEER_IU)]
                for h in range(PEER_HEADS):
                    rows_i = [ig * PEER_IU + t for t in range(PEER_IU)]
                    count = [packed_row(c0_ref[h, c, ii:ii + 1, :]) for ii in rows_i]
                    e0 = [packed_row(e0_ref[h, c, ii:ii + 1, :]) for ii in rows_i]
                    for jh in range(n_slabs):
                        js = slice(jh * PEER_JH // 2, (jh + 1) * PEER_JH // 2)
                        r1 = pltpu.bitcast(r1_ref[h, c, js, :], BF16)
                        e1 = pltpu.bitcast(e1_ref[h, c, js, :], BF16)
                        for t in range(PEER_IU):
                            gate = e1 * e0[t]
                            w[t][jh] = w[t][jh] + jnp.where(r1 < count[t], gate, jnp.zeros_like(gate))
                for t in range(PEER_IU):
                    for jh in range(n_slabs):
                        r0 = (ig * PEER_IU + t) * PEER_KEYS + jh * PEER_JH
                        rows = slice(r0, r0 + PEER_JH)
                        act = jax.nn.gelu(st_r[rows, cs] + anchor)
                        a_w[rows, cs] = act.astype(BF16) * w[t][jh]
            return carry

        lax.fori_loop(0, ncol, body, 0)

    @pl.when(j % 2 == 0)
    def _():
        run(st_a, st_b, a_b, a_a)

    @pl.when(j % 2 == 1)
    def _():
        run(st_b, st_a, a_a, a_b)

    @pl.when(j == pl.num_programs(1) - 1)
    def _():
        out = h_ref[...] + m_ref[...] * acc_scr[...].T
        if final_norm:
            out = out * lax.rsqrt(jnp.mean(out * out, axis=-1, keepdims=True) + RMS_EPS) * fg_ref[...]
        o_ref[...] = out


def peer_dense(x_bf, u_bf, vt_bf, c0, e0, r1, e1, h, gate, final_g):
    T, D = h.shape
    tq = min(T, 512)
    ncol = tq // LANES
    ni = PEER_TE // PEER_KEYS
    n_tiles = PEER_EXPERTS // PEER_TE
    final_norm = final_g is not None
    fg = (final_g if final_norm else jnp.ones((D,), F32)).reshape(1, D)
    tile = lambda j, lag: jnp.clip(j - lag, 0, n_tiles - 1)
    tok_rows = pl.BlockSpec((tq, D), lambda i, j: (i, 0))
    head_all = pl.BlockSpec((PEER_HEADS, ncol, PEER_KEYS // 2, LANES), lambda i, j: (0, i, 0, 0))
    head_blk = pl.BlockSpec((PEER_HEADS, ncol, ni, LANES), lambda i, j: (0, i, tile(j, 1), 0))
    return pl.pallas_call(
        functools.partial(_peer_dense_kernel, final_norm=final_norm),
        grid=(T // tq, n_tiles + 2),
        in_specs=[tok_rows,
                  pl.BlockSpec((PEER_TE, D), lambda i, j: (tile(j, 0), 0)),
                  pl.BlockSpec((D, PEER_TE), lambda i, j: (0, tile(j, 2))),
                  head_blk, head_blk, head_all, head_all,
                  tok_rows,
                  pl.BlockSpec((1, D), lambda i, j: (0, 0)),
                  pl.BlockSpec((1, D), lambda i, j: (0, 0))],
        out_specs=pl.BlockSpec((tq, D), lambda i, j: (i, 0)),
        out_shape=jax.ShapeDtypeStruct((T, D), F32),
        scratch_shapes=[pltpu.VMEM((D, tq), F32),
                        pltpu.VMEM((PEER_TE, tq), F32), pltpu.VMEM((PEER_TE, tq), F32),
                        pltpu.VMEM((PEER_TE, tq), BF16), pltpu.VMEM((PEER_TE, tq), BF16)],
        compiler_params=_cparams("parallel", "arbitrary"),
        name="peer_dense",
    )(x_bf, u_bf, vt_bf, c0, e0, r1, e1, h, gate.reshape(1, D), fg)


def peer_block(h, norm_g, shift, scale, gate, wq_bf, subkeys, u_bf, vt_bf, final_g=None):
    q, x_bf = nmm(h, norm_g, shift, scale, wq_bf)
    c0, e0, r1, e1 = peer_topk(q, subkeys)
    return peer_dense(x_bf, u_bf, vt_bf, c0, e0, r1, e1, h, gate, final_g)


def _softplus(x):
    return jnp.maximum(x, 0.0) + jnp.log(1.0 + jnp.exp(-jnp.abs(x)))


def _gdn_prep_kernel(p_ref, cw_ref, alog_ref, dtb_ref, q_ref, k_ref, v_ref, g_ref, gt_ref, *, width):
    tm = p_ref.shape[0]
    pos = lax.broadcasted_iota(jnp.int32, (tm, 1), 0) % width
    first = pos == 0
    last = pos == width - 1
    for cb in range(DN_CONV_CH // LANES):
        cs = slice(cb * LANES, (cb + 1) * LANES)
        x = p_ref[:, cs]
        prev = jnp.where(first, 0.0, pltpu.roll(x, 1, 0))
        nxt = jnp.where(last, 0.0, pltpu.roll(x, tm - 1, 0))
        c = _silu(prev * cw_ref[0:1, cs] + x * cw_ref[1:2, cs] + nxt * cw_ref[2:3, cs])
        hs = slice((cb % DN_HEADS) * LANES, (cb % DN_HEADS + 1) * LANES)
        if cb < DN_HEADS:
            q_ref[:, hs] = c * (lax.rsqrt(jnp.sum(c * c, axis=-1, keepdims=True) + 1e-6) * DN_DK ** -0.5)
        elif cb < 2 * DN_HEADS:
            k_ref[:, hs] = c * lax.rsqrt(jnp.sum(c * c, axis=-1, keepdims=True) + 1e-6)
        else:
            v_ref[:, hs] = c
    ab = p_ref[:, DN_IN_PAD - LANES:DN_IN_PAD]
    lane = lax.broadcasted_iota(jnp.int32, (1, LANES), 1)
    g = -jnp.exp(alog_ref[...]) * _softplus(ab + dtb_ref[...])
    gates = jnp.where(lane < 2 * DN_HEADS, g, jnp.where(lane < 4 * DN_HEADS, jax.nn.sigmoid(ab), 0.0))
    row = lax.broadcasted_iota(jnp.int32, (DN_CHUNK, DN_CHUNK), 0)
    col = lax.broadcasted_iota(jnp.int32, (DN_CHUNK, DN_CHUNK), 1)
    lower = (col <= row).astype(F32)
    upper = (col >= row).astype(F32)
    for ch in range(tm // DN_CHUNK):
        rs = slice(ch * DN_CHUNK, (ch + 1) * DN_CHUNK)
        pre = jnp.dot(lower, gates[rs], precision=HI, preferred_element_type=F32)
        suf = jnp.dot(upper, gates[rs], precision=HI, preferred_element_type=F32)
        g_ref[rs, :] = jnp.where(lane < DN_HEADS, pre, jnp.where(lane < 2 * DN_HEADS, suf, gates[rs]))
    gt_ref[...] = g_ref[...].T


def gdn_prep(p, conv_w, a_log, dt_bias, width):
    T = p.shape[0]
    tm = 256
    pad = lambda v: jnp.zeros((1, LANES), F32).at[0, :2 * DN_HEADS].set(v.reshape(-1))
    qkv = jax.ShapeDtypeStruct((T, DN_HEADS * DN_DK), F32)
    rows = lambda n: pl.BlockSpec((tm, n), lambda i: (i, 0))
    q, k, v, g, g_t = pl.pallas_call(
        functools.partial(_gdn_prep_kernel, width=width),
        grid=(T // tm,),
        in_specs=[rows(DN_IN_PAD),
                  pl.BlockSpec((3, DN_CONV_CH), lambda i: (0, 0)),
                  pl.BlockSpec((1, LANES), lambda i: (0, 0)),
                  pl.BlockSpec((1, LANES), lambda i: (0, 0))],
        out_specs=[rows(DN_HEADS * DN_DK)] * 3 + [rows(LANES), pl.BlockSpec((LANES, tm), lambda i: (0, i))],
        out_shape=[qkv, qkv, qkv, jax.ShapeDtypeStruct((T, LANES), F32),
                   jax.ShapeDtypeStruct((LANES, T), F32)],
        compiler_params=_cparams("parallel"),
        name="gdn_prep",
    )(p, conv_w, pad(a_log), pad(dt_bias))
    g_row = g_t[:2 * DN_HEADS].reshape(2 * DN_HEADS, T // DN_CHUNK, DN_CHUNK).transpose(1, 0, 2)
    return q, k, v, g, g_row


def _gdn_chunk_kernel(qf_ref, kf_ref, vf_ref, gf_ref, rf_ref, qb_ref, kb_ref, vb_ref, gb_ref, rb_ref,
                      s0_ref, of_ref, ob_ref, sfin_ref, s_scr):
    C = DN_CHUNK

    @pl.when(pl.program_id(0) == 0)
    def _():
        s_scr[...] = s0_ref[...]

    row = lax.broadcasted_iota(jnp.int32, (C, C), 0)
    col = lax.broadcasted_iota(jnp.int32, (C, C), 1)
    eye = (row == col).astype(F32)
    streams = ((qf_ref, kf_ref, vf_ref, gf_ref, rf_ref, of_ref), (qb_ref, kb_ref, vb_ref, gb_ref, rb_ref, ob_ref))
    chains = [(d, h) for d in range(2) for h in range(DN_HEADS)]
    mm = functools.partial(jnp.dot, preferred_element_type=F32)
    mm_nt = functools.partial(lax.dot_general, dimension_numbers=NT_DIMS, preferred_element_type=F32)
    mm_tn = functools.partial(lax.dot_general, dimension_numbers=TN_DIMS, preferred_element_type=F32)

    st = []
    for d, h in chains:
        q_ref, k_ref, v_ref, g_ref, r_ref, _ = streams[d]
        hs = slice(h * DN_DK, (h + 1) * DN_DK)
        gl = DN_HEADS * d + h
        inc = (col <= row) if d == 0 else (col >= row)
        strict = (col < row) if d == 0 else (col > row)
        gc = jnp.broadcast_to(g_ref[:, gl:gl + 1], (C, LANES))
        beta_b = jnp.broadcast_to(g_ref[:, 2 * DN_HEADS + gl:2 * DN_HEADS + gl + 1], (C, LANES))
        decay = jnp.where(inc, jnp.exp(jnp.where(inc, gc[:, :C] - r_ref[0, gl:gl + 1, :], 0.0)), 0.0)
        g_tot = gc[C - 1:C, :] if d == 0 else gc[0:1, :]
        k = k_ref[:, hs]
        k_beta = k * beta_b
        k_bf = k.astype(BF16)
        a = jnp.where(strict, mm_nt(k_beta.astype(BF16), k_bf) * decay, 0.0)
        e_gc = jnp.exp(gc)
        rhs = jnp.concatenate([v_ref[:, hs] * beta_b, k_beta * e_gc], axis=1).astype(BF16)
        q = q_ref[:, hs]
        attn = (mm_nt(q.astype(BF16), k_bf) * decay).astype(BF16)
        st.append(dict(hs=hs, a=a, rhs=rhs, attn=attn, qd=q * e_gc, g_tot=g_tot,
                       k_dec=(k * jnp.exp(g_tot - gc)).astype(BF16), t_inv=eye - a, p=a))
    for _ in range(5):
        for c in st:
            p_bf = c['p'].astype(BF16)
            c['p'] = mm(p_bf, p_bf)
        for c in st:
            c['t_inv'] = c['t_inv'] + mm(c['t_inv'].astype(BF16), c['p'].astype(BF16))
    for c in st:
        c['uw'] = mm(c['t_inv'].astype(BF16), c['rhs'])
    for c, (d, h) in zip(st, chains):
        c['s'] = s_scr[d, h]
        c['ws'] = mm(jnp.concatenate([c['uw'][:, DN_DK:], c['qd']], axis=0).astype(BF16), c['s'].astype(BF16))
    for c in st:
        c['v_new'] = (c['uw'][:, :DN_DK] - c['ws'][:C]).astype(BF16)
    for c, (d, h) in zip(st, chains):
        streams[d][5][:, c['hs']] = c['ws'][C:] + mm(c['attn'], c['v_new'])
    for c, (d, h) in zip(st, chains):
        s_scr[d, h] = c['s'] * jnp.exp(c['g_tot']) + mm_tn(c['k_dec'], c['v_new'])
    sfin_ref[...] = s_scr[...]


def gdn_chunk(q, k, v, g, g_row, s0):
    T, HD = q.shape
    n = T // DN_CHUNK
    fwd = lambda i: (i, 0)
    bwd = lambda i: (n - 1 - i, 0)
    qkv = lambda m: pl.BlockSpec((DN_CHUNK, HD), m)
    gsp = lambda m: pl.BlockSpec((DN_CHUNK, LANES), m)
    rsp = lambda m: pl.BlockSpec((1, 2 * DN_HEADS, DN_CHUNK), lambda i: m(i) + (0,))
    sspec = pl.BlockSpec((2, DN_HEADS, DN_DK, DN_DK), lambda i: (0, 0, 0, 0))
    return pl.pallas_call(
        _gdn_chunk_kernel,
        grid=(n,),
        in_specs=[qkv(fwd), qkv(fwd), qkv(fwd), gsp(fwd), rsp(fwd),
                  qkv(bwd), qkv(bwd), qkv(bwd), gsp(bwd), rsp(bwd), sspec],
        out_specs=[qkv(fwd), qkv(bwd), sspec],
        out_shape=[jax.ShapeDtypeStruct((T, HD), F32), jax.ShapeDtypeStruct((T, HD), F32),
                   jax.ShapeDtypeStruct((2, DN_HEADS, DN_DK, DN_DK), F32)],
        scratch_shapes=[pltpu.VMEM((2, DN_HEADS, DN_DK, DN_DK), F32)],
        compiler_params=_cparams("arbitrary"),
        name="gdn_chunk",
    )(q, k, v, g, g_row, q, k, v, g, g_row, s0)


def _gdn_out_kernel(of_ref, ob_ref, z_ref, ng_ref, w_ref, h_ref, m_ref, o_ref):
    ys = []
    for h in range(DN_HEADS):
        hs = slice(h * DN_DK, (h + 1) * DN_DK)
        o = of_ref[:, hs] + ob_ref[:, hs]
        y = o * lax.rsqrt(jnp.mean(o * o, axis=-1, keepdims=True) + RMS_EPS) * ng_ref[...]
        ys.append((y * _silu(z_ref[:, hs])).astype(BF16))
    out = jnp.dot(jnp.concatenate(ys, axis=1), w_ref[...], preferred_element_type=F32)
    o_ref[...] = h_ref[...] + m_ref[...] * out


def gdn_out(o_f, o_b, p, norm_g, w_out_bf, h, gate):
    T, D = h.shape
    tm = min(T, 512)
    rows = lambda n: pl.BlockSpec((tm, n), lambda i: (i, 0))
    return pl.pallas_call(
        _gdn_out_kernel,
        grid=(T // tm,),
        in_specs=[rows(D), rows(D),
                  pl.BlockSpec((tm, D), lambda i: (i, DN_CONV_CH // D)),
                  pl.BlockSpec((1, DN_DK), lambda i: (0, 0)),
                  pl.BlockSpec((D, D), lambda i: (0, 0)),
                  rows(D),
                  pl.BlockSpec((1, D), lambda i: (0, 0))],
        out_specs=rows(D),
        out_shape=jax.ShapeDtypeStruct((T, D), F32),
        compiler_params=_cparams("parallel"),
        name="gdn_out",
    )(o_f, o_b, p, norm_g.reshape(1, -1), w_out_bf, h, gate.reshape(1, D))


def kernel(x, c, ctx, c_ctx, ada_w, ada_b, norm1_g, norm2_g, e_w_in, e_conv_w, s5_lam_re, s5_lam_im,
           s5_log_dt, s5_b_re, s5_b_im, s5_c_re, s5_c_im, s5_d, s5_glu_w, s5_glu_b, e_w_out, o_w_in,
           dn_conv_w, dn_a_log, dn_dt_bias, dn_norm_g, o_w_out, peer_wq, peer_subkeys, peer_u, peer_v,
           final_g):
    D = D_MODEL
    h_lat = x[0]
    h_ctx = ctx[0]
    ctx_len = h_ctx.shape[0]
    cv = jnp.zeros((SUBLANES, D), F32).at[0].set(c[0]).at[1].set(c_ctx)
    mods = adaln(cv, ada_w, ada_b)
    depth = ada_w.shape[0]
    for l in range(depth):
        last = l == depth - 1
        i = l // 2
        m_lat = mods[l, 0].reshape(N_MOD, D)
        m_ctx = mods[l, 1].reshape(N_MOD, D)
        if l % 2 == 0:
            w_in = e_w_in[i].astype(BF16)
            p_lat, _ = nmm(h_lat, norm1_g[l], m_lat[0], m_lat[1], w_in)
            p_ctx, _ = nmm(h_ctx, norm1_g[l], m_ctx[0], m_ctx[1], w_in)
            u_lat = _to_segments(p_lat[:, 3 * CONV_CH:])
            u_ctx = _to_segments(p_ctx[:, 3 * CONV_CH:])
            y_lat, y_ctx = [], []
            zero = (jnp.zeros((SSM_STATE,), F32), jnp.zeros((SSM_STATE,), F32))
            for di in range(2):
                bd, cd, ab_re, ab_im = s5_params(s5_lam_re[i, di], s5_lam_im[i, di], s5_log_dt[i, di],
                                                 s5_b_re[i, di], s5_b_im[i, di], s5_c_re[i, di],
                                                 s5_c_im[i, di])
                yc, fin = s5_direction(u_ctx, bd, cd, ab_re, ab_im, zero, di == 1)
                yl, _ = s5_direction(u_lat, bd, cd, ab_re, ab_im, fin, di == 1)
                y_ctx.append(_from_segments(yc))
                y_lat.append(_from_segments(yl))
            glu_w = s5_glu_w[i].astype(BF16)
            w_out = e_w_out[i].astype(BF16)
            h_lat = even_post(p_lat, y_lat[0], y_lat[1], h_lat, e_conv_w[i], s5_d[i], glu_w, s5_glu_b[i],
                              w_out, m_lat[2], GRID_W)
            if not last:
                h_ctx = even_post(p_ctx, y_ctx[0], y_ctx[1], h_ctx, e_conv_w[i], s5_d[i], glu_w,
                                  s5_glu_b[i], w_out, m_ctx[2], ctx_len)
        else:
            w_in = jnp.pad(o_w_in[i], ((0, 0), (0, DN_IN_PAD - o_w_in.shape[2]))).astype(BF16)
            p_lat, _ = nmm(h_lat, norm1_g[l], m_lat[0], m_lat[1], w_in)
            p_ctx, _ = nmm(h_ctx, norm1_g[l], m_ctx[0], m_ctx[1], w_in)
            qc, kc, vc, gc, rc = gdn_prep(p_ctx, dn_conv_w[i], dn_a_log[i], dn_dt_bias[i], ctx_len)
            ql, kl, vl, gl, rl = gdn_prep(p_lat, dn_conv_w[i], dn_a_log[i], dn_dt_bias[i], GRID_W)
            zero = jnp.zeros((2, DN_HEADS, DN_DK, DN_DK), F32)
            oc_f, oc_b, s_ctx = gdn_chunk(qc, kc, vc, gc, rc, zero)
            ol_f, ol_b, _ = gdn_chunk(ql, kl, vl, gl, rl, s_ctx)
            w_out = o_w_out[i].astype(BF16)
            h_lat = gdn_out(ol_f, ol_b, p_lat, dn_norm_g[i], w_out, h_lat, m_lat[2])
            if not last:
                h_ctx = gdn_out(oc_f, oc_b, p_ctx, dn_norm_g[i], w_out, h_ctx, m_ctx[2])
        wq = peer_wq[l].astype(BF16)
        u_bf = peer_u[l].astype(BF16)
        vt_bf = peer_v[l].T.astype(BF16)
        h_lat = peer_block(h_lat, norm2_g[l], m_lat[3], m_lat[4], m_lat[5], wq, peer_subkeys[l], u_bf,
                           vt_bf, final_g if last else None)
        if not last:
            h_ctx = peer_block(h_ctx, norm2_g[l], m_ctx[3], m_ctx[4], m_ctx[5], wq, peer_subkeys[l], u_bf,
                               vt_bf)
    return h_lat[None]
```

```python
import functools
import math

import jax
import jax.numpy as jnp
from jax import lax
from jax.experimental import pallas as pl
from jax.experimental.pallas import tpu as pltpu

F32 = jnp.float32
BF16 = jnp.bfloat16
HI = lax.Precision.HIGHEST

D_MODEL = 1024
GRID_W = 64
RMS_EPS = 1e-6
N_MOD = 6
CONV_CH = 512
SSM_CH = 512
SSM_H = 16
SSM_G = 32
SSM_P = 64
SSM_STATE = SSM_G * SSM_P
DN_HEADS = 8
DN_DK = 128
DN_CHUNK = 64
DN_CONV_CH = 3072
DN_IN_PAD = 4224
PEER_HEADS = 8
PEER_KEYS = 128
PEER_TOPK = 16
PEER_EXPERTS = PEER_KEYS * PEER_KEYS

SUBLANES = 8
LANES = 128
VMEM_LIMIT = 52 * 1024 * 1024

NT_DIMS = (((1,), (1,)), ((), ()))
TN_DIMS = (((0,), (0,)), ((), ()))


def _cparams(*sem):
    return pltpu.CompilerParams(dimension_semantics=sem, vmem_limit_bytes=VMEM_LIMIT)


def _silu(x):
    return x * jax.nn.sigmoid(x)


def _adaln_kernel(c_ref, w_ref, b_ref, o_ref):
    sc = _silu(c_ref[...])
    o_ref[0] = jnp.dot(sc, w_ref[0], precision=HI, preferred_element_type=F32) + b_ref[0]


def adaln(cv, ada_w, ada_b):
    L, D, N = ada_w.shape
    tn = 1536
    return pl.pallas_call(
        _adaln_kernel,
        grid=(L, N // tn),
        in_specs=[pl.BlockSpec((SUBLANES, D), lambda l, j: (0, 0)),
                  pl.BlockSpec((1, D, tn), lambda l, j: (l, 0, j)),
                  pl.BlockSpec((1, 1, tn), lambda l, j: (l, 0, j))],
        out_specs=pl.BlockSpec((1, SUBLANES, tn), lambda l, j: (l, 0, j)),
        out_shape=jax.ShapeDtypeStruct((L, SUBLANES, N), F32),
        compiler_params=_cparams("parallel", "parallel"),
        name="adaln",
    )(cv, ada_w, ada_b.reshape(L, 1, N))


def _nmm_kernel(x_ref, g_ref, sh_ref, sc_ref, w_ref, o_ref, a_ref):
    x = x_ref[...]
    y = x * lax.rsqrt(jnp.mean(x * x, axis=-1, keepdims=True) + RMS_EPS)
    a = ((y * g_ref[...]) * (1.0 + sc_ref[...]) + sh_ref[...]).astype(BF16)
    a_ref[...] = a
    o_ref[...] = jnp.dot(a, w_ref[...], preferred_element_type=F32)


def nmm(x, g, shift, scale, w_bf):
    T, D = x.shape
    N = w_bf.shape[1]
    tm = min(T, 512)
    row = lambda v: v.reshape(1, D)
    return pl.pallas_call(
        _nmm_kernel,
        grid=(T // tm,),
        in_specs=[pl.BlockSpec((tm, D), lambda i: (i, 0)),
                  pl.BlockSpec((1, D), lambda i: (0, 0)),
                  pl.BlockSpec((1, D), lambda i: (0, 0)),
                  pl.BlockSpec((1, D), lambda i: (0, 0)),
                  pl.BlockSpec((D, N), lambda i: (0, 0))],
        out_specs=[pl.BlockSpec((tm, N), lambda i: (i, 0)),
                   pl.BlockSpec((tm, D), lambda i: (i, 0))],
        out_shape=[jax.ShapeDtypeStruct((T, N), F32),
                   jax.ShapeDtypeStruct((T, D), BF16)],
        compiler_params=_cparams("parallel"),
        name="nmm",
    )(x, row(g), row(shift), row(scale), w_bf)


S5_BLOCKS = 4
S5_COLS = SSM_STATE // S5_BLOCKS


def _s5_kernel(u_ref, bd_ref, cd_ref, a_ref, init_ref, y_ref, ends_ref, s_scr, st_scr,
               *, reverse, tj, emit_y):
    @pl.when(pl.program_id(0) == 0)
    def _():
        st_scr[...] = init_ref[...]

    ch = SSM_CH // S5_BLOCKS
    for cc in range(S5_BLOCKS):
        re = slice(cc * S5_COLS, (cc + 1) * S5_COLS)
        im = slice(SSM_STATE + cc * S5_COLS, SSM_STATE + (cc + 1) * S5_COLS)
        bu = jnp.dot(u_ref[:, cc * ch:(cc + 1) * ch].astype(BF16), bd_ref[cc], preferred_element_type=F32)
        s_scr[:, re] = bu[:, :S5_COLS]
        s_scr[:, im] = bu[:, S5_COLS:]
        ar = a_ref[:, re]
        ai = a_ref[:, im]

        def body(jj, carry, re=re, im=im, ar=ar, ai=ai):
            sr, si = carry
            j = (tj - 1 - jj) if reverse else jj
            rows = pl.ds(pl.multiple_of(j * SUBLANES, SUBLANES), SUBLANES)
            nr = ar * sr - ai * si + s_scr[rows, re]
            ni = ar * si + ai * sr + s_scr[rows, im]
            s_scr[rows, re] = nr
            s_scr[rows, im] = ni
            return nr, ni

        sr, si = lax.fori_loop(0, tj, body, (st_scr[:, re], st_scr[:, im]))
        st_scr[:, re] = sr
        st_scr[:, im] = si
        if emit_y:
            y_ref[:, cc * ch:(cc + 1) * ch] = (
                jnp.dot(s_scr[:, re].astype(BF16), cd_ref[cc, :S5_COLS], preferred_element_type=F32)
                + jnp.dot(s_scr[:, im].astype(BF16), cd_ref[cc, S5_COLS:], preferred_element_type=F32))
    if not emit_y:
        y_ref[...] = jnp.zeros_like(y_ref)
    ends_ref[...] = st_scr[...]


def s5_pass(u_perm, bd, cd, a_t, init, reverse, emit_y):
    T, C = u_perm.shape
    tt = min(T, 512)
    nt = T // tt
    idx = (lambda i: (nt - 1 - i, 0)) if reverse else (lambda i: (i, 0))
    kern = functools.partial(_s5_kernel, reverse=reverse, tj=tt // SUBLANES, emit_y=emit_y)
    yt = tt if emit_y else SUBLANES
    return pl.pallas_call(
        kern,
        grid=(nt,),
        in_specs=[pl.BlockSpec((tt, C), idx),
                  pl.BlockSpec((S5_BLOCKS, C // S5_BLOCKS, 2 * S5_COLS), lambda i: (0, 0, 0)),
                  pl.BlockSpec((S5_BLOCKS, 2 * S5_COLS, C // S5_BLOCKS), lambda i: (0, 0, 0)),
                  pl.BlockSpec((SUBLANES, 2 * SSM_STATE), lambda i: (0, 0)),
                  pl.BlockSpec((SUBLANES, 2 * SSM_STATE), lambda i: (0, 0))],
        out_specs=[pl.BlockSpec((yt, C), idx if emit_y else (lambda i: (0, 0))),
                   pl.BlockSpec((SUBLANES, 2 * SSM_STATE), lambda i: (0, 0))],
        out_shape=[jax.ShapeDtypeStruct((T if emit_y else SUBLANES, C), F32),
                   jax.ShapeDtypeStruct((SUBLANES, 2 * SSM_STATE), F32)],
        scratch_shapes=[pltpu.VMEM((tt, 2 * SSM_STATE), F32),
                        pltpu.VMEM((SUBLANES, 2 * SSM_STATE), F32)],
        compiler_params=_cparams("arbitrary"),
        name="s5_scan",
    )(u_perm, bd, cd, a_t, init)


def _cmul(ar, ai, br, bi):
    return ar * br - ai * bi, ar * bi + ai * br


def s5_direction(u_perm, bd, cd, ab_re, ab_im, s0, reverse):
    T = u_perm.shape[0]
    tseg = T // SUBLANES
    a_t = jnp.broadcast_to(jnp.concatenate([ab_re, ab_im])[None, :], (SUBLANES, 2 * SSM_STATE))
    _, ends = s5_pass(u_perm, bd, cd, a_t, jnp.zeros((SUBLANES, 2 * SSM_STATE), F32), reverse, False)
    pr, pi = ab_re, ab_im
    for _ in range(int(math.log2(tseg))):
        pr, pi = _cmul(pr, pi, pr, pi)
    er, ei = ends[:, :SSM_STATE], ends[:, SSM_STATE:]
    cr, ci = s0
    starts = [None] * SUBLANES
    order = range(SUBLANES - 1, -1, -1) if reverse else range(SUBLANES)
    for r in order:
        starts[r] = jnp.concatenate([cr, ci])
        nr, ni = _cmul(pr, pi, cr, ci)
        cr, ci = nr + er[r], ni + ei[r]
    y, _ = s5_pass(u_perm, bd, cd, a_t, jnp.stack(starts), reverse, True)
    return y, (cr, ci)


def s5_params(lam_re, lam_im, log_dt, b_re, b_im, c_re, c_im):
    lam_re = jnp.minimum(lam_re, -1e-4)
    dt = jnp.exp(log_dt)[:, None]
    mag = jnp.exp(lam_re * dt)
    ab_re = mag * jnp.cos(lam_im * dt)
    ab_im = mag * jnp.sin(lam_im * dt)
    den = lam_re * lam_re + lam_im * lam_im
    f_re = ((ab_re - 1) * lam_re + ab_im * lam_im) / den
    f_im = (ab_im * lam_re - (ab_re - 1) * lam_im) / den
    bb_re = f_re[..., None] * b_re - f_im[..., None] * b_im
    bb_im = f_re[..., None] * b_im + f_im[..., None] * b_re
    gb = SSM_G // S5_BLOCKS
    eye = jnp.eye(gb, dtype=F32)
    blk_in = lambda t: jnp.einsum('bgph,gk->bghkp', t.reshape(S5_BLOCKS, gb, SSM_P, SSM_H),
                                  eye).reshape(S5_BLOCKS, gb * SSM_H, gb * SSM_P)
    blk_out = lambda t: jnp.einsum('bghp,gk->bgpkh', t.reshape(S5_BLOCKS, gb, SSM_H, SSM_P),
                                   eye).reshape(S5_BLOCKS, gb * SSM_P, gb * SSM_H)
    bd = jnp.concatenate([blk_in(bb_re), blk_in(bb_im)], axis=2).astype(BF16)
    cd = jnp.concatenate([blk_out(c_re), -blk_out(c_im)], axis=1).astype(BF16)
    return bd, cd, ab_re.reshape(-1), ab_im.reshape(-1)


def _to_segments(u):
    T, C = u.shape
    return u.reshape(SUBLANES, T // SUBLANES, C).transpose(1, 0, 2).reshape(T, C)


def _from_segments(y):
    T, C = y.shape
    return y.reshape(T // SUBLANES, SUBLANES, C).transpose(1, 0, 2).reshape(T, C)


def _conv3(x, w_ref, width):
    tm = x.shape[0]
    pos = lax.broadcasted_iota(jnp.int32, (tm, 1), 0) % width
    prev = jnp.where(pos == 0, 0.0, pltpu.roll(x, 1, 0))
    nxt = jnp.where(pos == width - 1, 0.0, pltpu.roll(x, tm - 1, 0))
    return prev * w_ref[0:1, :] + x * w_ref[1:2, :] + nxt * w_ref[2:3, :]


def _even_post_kernel(p_ref, yf_ref, yb_ref, h_ref, cw_ref, d_ref, gw_ref, gb_ref, wo_ref, m_ref,
                      o_ref, *, width):
    u = p_ref[:, 0:CONV_CH]
    g_in = p_ref[:, CONV_CH:2 * CONV_CH]
    g_out = p_ref[:, 2 * CONV_CH:3 * CONV_CH]
    branch_a = g_out * _conv3(g_in * u, cw_ref, width)
    s = p_ref[:, 3 * CONV_CH:] * d_ref[...] + yf_ref[...] + yb_ref[...]
    y = jax.nn.gelu(s)
    gate = jnp.dot(y.astype(BF16), gw_ref[...], preferred_element_type=F32) + gb_ref[...]
    branch_b = y * jax.nn.sigmoid(gate)
    out = (jnp.dot(branch_a.astype(BF16), wo_ref[0:CONV_CH, :], preferred_element_type=F32)
           + jnp.dot(branch_b.astype(BF16), wo_ref[CONV_CH:, :], preferred_element_type=F32))
    o_ref[...] = h_ref[...] + m_ref[...] * out


def even_post(p, yf, yb, h, conv_w, d_skip, glu_w_bf, glu_b, w_out_bf, gate, width):
    T, D = h.shape
    tm = min(T, 512)
    full = lambda shape: pl.BlockSpec(shape, lambda i: (0, 0))
    rows = lambda n: pl.BlockSpec((tm, n), lambda i: (i, 0))
    return pl.pallas_call(
        functools.partial(_even_post_kernel, width=width),
        grid=(T // tm,),
        in_specs=[rows(4 * CONV_CH), rows(SSM_CH), rows(SSM_CH), rows(D),
                  full((3, CONV_CH)), full((1, SSM_CH)), full((SSM_CH, SSM_CH)), full((1, SSM_CH)),
                  full((2 * CONV_CH, D)), full((1, D))],
        out_specs=rows(D),
        out_shape=jax.ShapeDtypeStruct((T, D), F32),
        compiler_params=_cparams("parallel"),
        name="even_post",
    )(p, yf, yb, h, conv_w, d_skip.reshape(1, -1), glu_w_bf, glu_b.reshape(1, -1), w_out_bf,
      gate.reshape(1, D))


def _extract_top(w, n, want_rank=False):
    vals = []
    rank = jnp.full(w.shape, float(n), F32) if want_rank else None
    for k in range(n):
        m = jnp.max(w, axis=0, keepdims=True)
        vals.append(m)
        hit = w == m
        if want_rank:
            rank = jnp.where(hit, float(k), rank)
        w = jnp.where(hit, -jnp.inf, w)
    return vals, rank


_PEER_PAIRS = [(k, l) for k in range(PEER_TOPK + 1) for l in range(PEER_TOPK + 1)
               if (k + 1) * (l + 1) <= PEER_TOPK + 1]
_PEER_CAND_ROWS = -(-len(_PEER_PAIRS) // SUBLANES) * SUBLANES


def _peer_topk_kernel(q_ref, sk_ref, c0_ref, e0_ref, r1_ref, e1_ref, cand_scr):
    tq = q_ref.shape[0]
    cand_scr[...] = jnp.full(cand_scr.shape, -jnp.inf, F32)
    for h in range(PEER_HEADS):
        sc = []
        for s in range(2):
            col = (2 * h + s) * LANES
            sc.append(lax.dot_general(sk_ref[s], q_ref[:, col:col + LANES], NT_DIMS,
                                      precision=HI, preferred_element_type=F32))
        top0, _ = _extract_top(sc[0], PEER_TOPK + 1)
        top1, rank1 = _extract_top(sc[1], PEER_TOPK + 1, want_rank=True)
        for r, (k, l) in enumerate(_PEER_PAIRS):
            cand_scr[r:r + 1, :] = top0[k] + top1[l]
        best, _ = _extract_top(cand_scr[...], PEER_TOPK + 1)
        z = jnp.zeros_like(best[0])
        for k in range(PEER_TOPK):
            z = z + jnp.exp(best[k] - best[0])
        t0 = 0.5 * (best[PEER_TOPK - 1] + best[PEER_TOPK]) - sc[0]
        count = jnp.zeros_like(t0)
        for k in range(PEER_TOPK):
            count = count + jnp.where(top1[k] >= t0, 1.0, 0.0)
        e0 = jnp.exp(sc[0] - top0[0]) / z
        rank1 = pltpu.bitcast(rank1.astype(BF16), jnp.uint32)
        e1 = pltpu.bitcast(jnp.exp(sc[1] - top1[0]).astype(BF16), jnp.uint32)
        for c in range(tq // LANES):
            cs = slice(c * LANES, (c + 1) * LANES)
            c0_ref[h, c] = count[:, cs]
            e0_ref[h, c] = e0[:, cs]
            r1_ref[h, c] = rank1[:, cs]
            e1_ref[h, c] = e1[:, cs]


def peer_topk(q, subkeys):
    T = q.shape[0]
    tq = min(T, 256)
    table = lambda rows, dt: jax.ShapeDtypeStruct((PEER_HEADS, T // LANES, rows, LANES), dt)
    bspec = lambda rows: pl.BlockSpec((PEER_HEADS, tq // LANES, rows, LANES), lambda i: (0, i, 0, 0))
    return pl.pallas_call(
        _peer_topk_kernel,
        grid=(T // tq,),
        in_specs=[pl.BlockSpec((tq, q.shape[1]), lambda i: (i, 0)),
                  pl.BlockSpec((2, PEER_KEYS, LANES), lambda i: (0, 0, 0))],
        out_specs=[bspec(PEER_KEYS), bspec(PEER_KEYS), bspec(PEER_KEYS // 2), bspec(PEER_KEYS // 2)],
        out_shape=[table(PEER_KEYS, F32), table(PEER_KEYS, F32),
                   table(PEER_KEYS // 2, jnp.uint32), table(PEER_KEYS // 2, jnp.uint32)],
        scratch_shapes=[pltpu.VMEM((_PEER_CAND_ROWS, tq), F32)],
        compiler_params=_cparams("parallel"),
        name="peer_topk",
    )(q, subkeys)


PEER_TE = 1024
PEER_JH = 16
PEER_IU = 2


def _peer_dense_kernel(xt_ref, u_ref, vt_ref, c0_ref, e0_ref, r1_ref, e1_ref, h_ref, m_ref, fg_ref, o_ref,
                       acc_scr, st_a, st_b, a_a, a_b, *, final_norm, n_tiles):
    j = pl.program_id(1)
    d, tq = xt_ref.shape
    ncol = tq // LANES
    e_rows = PEER_TE // ncol
    d_rows = d // ncol

    def packed_row(row):
        return jnp.broadcast_to(row, (PEER_JH, LANES)).astype(BF16)

    @pl.when(j == 0)
    def _():
        acc_scr[...] = jnp.zeros_like(acc_scr)

    def run(parity, do_score, do_mask, do_acc):
        st_w, st_r = (st_a, st_b) if parity == 0 else (st_b, st_a)
        a_w, a_r = (a_b, a_a) if parity == 0 else (a_a, a_b)

        def body(c, carry):
            er = pl.ds(pl.multiple_of(c * e_rows, e_rows), e_rows)
            dr = pl.ds(pl.multiple_of(c * d_rows, d_rows), d_rows)
            half = tq // 2

            def mxu_piece(k):
                ts = slice((k % 2) * half, (k % 2 + 1) * half)
                if k < 2 and do_score:
                    res = jnp.dot(u_ref[er, :], xt_ref[:, ts], preferred_element_type=F32)
                    st_w[er, ts] = res
                elif k >= 2 and do_acc:
                    res = acc_scr[dr, ts] + jnp.dot(vt_ref[dr, :], a_r[:, ts], preferred_element_type=F32)
                    acc_scr[dr, ts] = res
                else:
                    return jnp.zeros((1, LANES), F32)
                return jnp.minimum(jnp.abs(res[0:1, 0:LANES]), 0.0)

            cs = pl.ds(pl.multiple_of(c * LANES, LANES), LANES)
            n_slabs = PEER_KEYS // PEER_JH
            for ig in range(PEER_TE // PEER_KEYS // PEER_IU):
                anchor = mxu_piece(ig)
                if not do_mask:
                    continue
                w = [[jnp.zeros((PEER_JH, LANES), BF16) for _ in range(n_slabs)] for _ in range(PEER_IU)]
                for h in range(PEER_HEADS):
                    rows_i = [ig * PEER_IU + t for t in range(PEER_IU)]
                    count = [packed_row(c0_ref[h, c, ii:ii + 1, :]) for ii in rows_i]
                    e0 = [packed_row(e0_ref[h, c, ii:ii + 1, :]) for ii in rows_i]
                    for jh in range(n_slabs):
                        js = slice(jh * PEER_JH // 2, (jh + 1) * PEER_JH // 2)
                        r1 = pltpu.bitcast(r1_ref[h, c, js, :], BF16)
                        e1 = pltpu.bitcast(e1_ref[h, c, js, :], BF16)
                        for t in range(PEER_IU):
                            gate = e1 * e0[t]
                            w[t][jh] = w[t][jh] + jnp.where(r1 < count[t], gate, jnp.zeros_like(gate))
                for t in range(PEER_IU):
                    for jh in range(n_slabs):
                        r0 = (ig * PEER_IU + t) * PEER_KEYS + jh * PEER_JH
                        rows = slice(r0, r0 + PEER_JH)
                        act = jax.nn.gelu((st_r[rows, cs] + anchor).astype(BF16))
                        a_w[rows, cs] = act * w[t][jh]
            return carry

        lax.fori_loop(0, ncol, body, 0)

    edge_steps = {0: (True, False, False), 1: (True, True, False),
                  n_tiles: (False, True, True), n_tiles + 1: (False, False, True)}
    for step, stages in edge_steps.items():
        pl.when(j == step)(functools.partial(run, step % 2, *stages))
    steady = (j >= 2) & (j < n_tiles)
    for parity in range(2):
        pl.when(steady & (j % 2 == parity))(functools.partial(run, parity, True, True, True))

    @pl.when(j == pl.num_programs(1) - 1)
    def _():
        out = h_ref[...] + m_ref[...] * acc_scr[...].T
        if final_norm:
            out = out * lax.rsqrt(jnp.mean(out * out, axis=-1, keepdims=True) + RMS_EPS) * fg_ref[...]
        o_ref[...] = out


def peer_dense(xt_bf, u_bf, vt_bf, c0, e0, r1, e1, h, gate, final_g):
    T, D = h.shape
    tq = min(T, 512)
    ncol = tq // LANES
    ni = PEER_TE // PEER_KEYS
    n_tiles = PEER_EXPERTS // PEER_TE
    final_norm = final_g is not None
    fg = (final_g if final_norm else jnp.ones((D,), F32)).reshape(1, D)
    tile = lambda j, lag: jnp.clip(j - lag, 0, n_tiles - 1)
    tok_rows = pl.BlockSpec((tq, D), lambda i, j: (i, 0))
    head_all = pl.BlockSpec((PEER_HEADS, ncol, PEER_KEYS // 2, LANES), lambda i, j: (0, i, 0, 0))
    head_blk = pl.BlockSpec((PEER_HEADS, ncol, ni, LANES), lambda i, j: (0, i, tile(j, 1), 0))
    return pl.pallas_call(
        functools.partial(_peer_dense_kernel, final_norm=final_norm, n_tiles=n_tiles),
        grid=(T // tq, n_tiles + 2),
        in_specs=[pl.BlockSpec((D, tq), lambda i, j: (0, i)),
                  pl.BlockSpec((PEER_TE, D), lambda i, j: (tile(j, 0), 0)),
                  pl.BlockSpec((D, PEER_TE), lambda i, j: (0, tile(j, 2))),
                  head_blk, head_blk, head_all, head_all,
                  tok_rows,
                  pl.BlockSpec((1, D), lambda i, j: (0, 0)),
                  pl.BlockSpec((1, D), lambda i, j: (0, 0))],
        out_specs=pl.BlockSpec((tq, D), lambda i, j: (i, 0)),
        out_shape=jax.ShapeDtypeStruct((T, D), F32),
        scratch_shapes=[pltpu.VMEM((D, tq), F32),
                        pltpu.VMEM((PEER_TE, tq), F32), pltpu.VMEM((PEER_TE, tq), F32),
                        pltpu.VMEM((PEER_TE, tq), BF16), pltpu.VMEM((PEER_TE, tq), BF16)],
        compiler_params=_cparams("parallel", "arbitrary"),
        name="peer_dense",
    )(xt_bf, u_bf, vt_bf, c0, e0, r1, e1, h, gate.reshape(1, D), fg)


def peer_block(h, norm_g, shift, scale, gate, wq_bf, subkeys, u_bf, vt_bf, final_g=None):
    q, x_bf = nmm(h, norm_g, shift, scale, wq_bf)
    c0, e0, r1, e1 = peer_topk(q, subkeys)
    return peer_dense(x_bf.T, u_bf, vt_bf, c0, e0, r1, e1, h, gate, final_g)


def _softplus(x):
    return jnp.maximum(x, 0.0) + jnp.log(1.0 + jnp.exp(-jnp.abs(x)))


def _gdn_prep_kernel(p_ref, cw_ref, alog_ref, dtb_ref, q_ref, k_ref, v_ref, g_ref, gt_ref, *, width):
    tm = p_ref.shape[0]
    pos = lax.broadcasted_iota(jnp.int32, (tm, 1), 0) % width
    first = pos == 0
    last = pos == width - 1
    for cb in range(DN_CONV_CH // LANES):
        cs = slice(cb * LANES, (cb + 1) * LANES)
        x = p_ref[:, cs]
        prev = jnp.where(first, 0.0, pltpu.roll(x, 1, 0))
        nxt = jnp.where(last, 0.0, pltpu.roll(x, tm - 1, 0))
        c = _silu(prev * cw_ref[0:1, cs] + x * cw_ref[1:2, cs] + nxt * cw_ref[2:3, cs])
        hs = slice((cb % DN_HEADS) * LANES, (cb % DN_HEADS + 1) * LANES)
        if cb < DN_HEADS:
            q_ref[:, hs] = c * (lax.rsqrt(jnp.sum(c * c, axis=-1, keepdims=True) + 1e-6) * DN_DK ** -0.5)
        elif cb < 2 * DN_HEADS:
            k_ref[:, hs] = c * lax.rsqrt(jnp.sum(c * c, axis=-1, keepdims=True) + 1e-6)
        else:
            v_ref[:, hs] = c
    ab = p_ref[:, DN_IN_PAD - LANES:DN_IN_PAD]
    lane = lax.broadcasted_iota(jnp.int32, (1, LANES), 1)
    g = -jnp.exp(alog_ref[...]) * _softplus(ab + dtb_ref[...])
    gates = jnp.where(lane < 2 * DN_HEADS, g, jnp.where(lane < 4 * DN_HEADS, jax.nn.sigmoid(ab), 0.0))
    row = lax.broadcasted_iota(jnp.int32, (DN_CHUNK, DN_CHUNK), 0)
    col = lax.broadcasted_iota(jnp.int32, (DN_CHUNK, DN_CHUNK), 1)
    lower = (col <= row).astype(F32)
    upper = (col >= row).astype(F32)
    for ch in range(tm // DN_CHUNK):
        rs = slice(ch * DN_CHUNK, (ch + 1) * DN_CHUNK)
        pre = jnp.dot(lower, gates[rs], precision=HI, preferred_element_type=F32)
        suf = jnp.dot(upper, gates[rs], precision=HI, preferred_element_type=F32)
        g_ref[rs, :] = jnp.where(lane < DN_HEADS, pre, jnp.where(lane < 2 * DN_HEADS, suf, gates[rs]))
    gt_ref[...] = g_ref[...].T


def gdn_prep(p, conv_w, a_log, dt_bias, width):
    T = p.shape[0]
    tm = 256
    pad = lambda v: jnp.zeros((1, LANES), F32).at[0, :2 * DN_HEADS].set(v.reshape(-1))
    qkv = jax.ShapeDtypeStruct((T, DN_HEADS * DN_DK), F32)
    rows = lambda n: pl.BlockSpec((tm, n), lambda i: (i, 0))
    q, k, v, g, g_t = pl.pallas_call(
        functools.partial(_gdn_prep_kernel, width=width),
        grid=(T // tm,),
        in_specs=[rows(DN_IN_PAD),
                  pl.BlockSpec((3, DN_CONV_CH), lambda i: (0, 0)),
                  pl.BlockSpec((1, LANES), lambda i: (0, 0)),
                  pl.BlockSpec((1, LANES), lambda i: (0, 0))],
        out_specs=[rows(DN_HEADS * DN_DK)] * 3 + [rows(LANES), pl.BlockSpec((LANES, tm), lambda i: (0, i))],
        out_shape=[qkv, qkv, qkv, jax.ShapeDtypeStruct((T, LANES), F32),
                   jax.ShapeDtypeStruct((LANES, T), F32)],
        compiler_params=_cparams("parallel"),
        name="gdn_prep",
    )(p, conv_w, pad(a_log), pad(dt_bias))
    g_row = g_t[:2 * DN_HEADS].reshape(2 * DN_HEADS, T // DN_CHUNK, DN_CHUNK).transpose(1, 0, 2)
    return q, k, v, g, g_row


def _gdn_chunk_kernel(qf_ref, kf_ref, vf_ref, gf_ref, rf_ref, qb_ref, kb_ref, vb_ref, gb_ref, rb_ref,
                      s0_ref, of_ref, ob_ref, sfin_ref, s_scr):
    C = DN_CHUNK

    @pl.when(pl.program_id(0) == 0)
    def _():
        s_scr[...] = s0_ref[...]

    row = lax.broadcasted_iota(jnp.int32, (C, C), 0)
    col = lax.broadcasted_iota(jnp.int32, (C, C), 1)
    eye = (row == col).astype(F32)
    streams = ((qf_ref, kf_ref, vf_ref, gf_ref, rf_ref, of_ref), (qb_ref, kb_ref, vb_ref, gb_ref, rb_ref, ob_ref))
    chains = [(d, h) for d in range(2) for h in range(DN_HEADS)]
    mm = functools.partial(jnp.dot, preferred_element_type=F32)
    mm_nt = functools.partial(lax.dot_general, dimension_numbers=NT_DIMS, preferred_element_type=F32)
    mm_tn = functools.partial(lax.dot_general, dimension_numbers=TN_DIMS, preferred_element_type=F32)

    st = []
    for d, h in chains:
        q_ref, k_ref, v_ref, g_ref, r_ref, _ = streams[d]
        hs = slice(h * DN_DK, (h + 1) * DN_DK)
        gl = DN_HEADS * d + h
        inc = (col <= row) if d == 0 else (col >= row)
        strict = (col < row) if d == 0 else (col > row)
        gc = jnp.broadcast_to(g_ref[:, gl:gl + 1], (C, LANES))
        beta_b = jnp.broadcast_to(g_ref[:, 2 * DN_HEADS + gl:2 * DN_HEADS + gl + 1], (C, LANES))
        decay = jnp.where(inc, jnp.exp(jnp.where(inc, gc[:, :C] - r_ref[0, gl:gl + 1, :], 0.0)), 0.0)
        g_tot = gc[C - 1:C, :] if d == 0 else gc[0:1, :]
        k = k_ref[:, hs]
        k_beta = k * beta_b
        k_bf = k.astype(BF16)
        a = jnp.where(strict, mm_nt(k_beta.astype(BF16), k_bf) * decay, 0.0)
        e_gc = jnp.exp(gc)
        rhs = jnp.concatenate([v_ref[:, hs] * beta_b, k_beta * e_gc], axis=1).astype(BF16)
        q = q_ref[:, hs]
        attn = (mm_nt(q.astype(BF16), k_bf) * decay).astype(BF16)
        st.append(dict(hs=hs, a=a, rhs=rhs, attn=attn, qd=q * e_gc, g_tot=g_tot,
                       k_dec=(k * jnp.exp(g_tot - gc)).astype(BF16), t_inv=eye - a, p=a))
    for _ in range(5):
        for c in st:
            p_bf = c['p'].astype(BF16)
            c['p'] = mm(p_bf, p_bf)
        for c in st:
            c['t_inv'] = c['t_inv'] + mm(c['t_inv'].astype(BF16), c['p'].astype(BF16))
    for c in st:
        c['uw'] = mm(c['t_inv'].astype(BF16), c['rhs'])
    for c, (d, h) in zip(st, chains):
        c['s'] = s_scr[d, h]
        c['ws'] = mm(jnp.concatenate([c['uw'][:, DN_DK:], c['qd']], axis=0).astype(BF16), c['s'].astype(BF16))
    for c in st:
        c['v_new'] = (c['uw'][:, :DN_DK] - c['ws'][:C]).astype(BF16)
    for c, (d, h) in zip(st, chains):
        streams[d][5][:, c['hs']] = c['ws'][C:] + mm(c['attn'], c['v_new'])
    for c, (d, h) in zip(st, chains):
        s_scr[d, h] = c['s'] * jnp.exp(c['g_tot']) + mm_tn(c['k_dec'], c['v_new'])
    sfin_ref[...] = s_scr[...]


def gdn_chunk(q, k, v, g, g_row, s0):
    T, HD = q.shape
    n = T // DN_CHUNK
    fwd = lambda i: (i, 0)
    bwd = lambda i: (n - 1 - i, 0)
    qkv = lambda m: pl.BlockSpec((DN_CHUNK, HD), m)
    gsp = lambda m: pl.BlockSpec((DN_CHUNK, LANES), m)
    rsp = lambda m: pl.BlockSpec((1, 2 * DN_HEADS, DN_CHUNK), lambda i: m(i) + (0,))
    sspec = pl.BlockSpec((2, DN_HEADS, DN_DK, DN_DK), lambda i: (0, 0, 0, 0))
    return pl.pallas_call(
        _gdn_chunk_kernel,
        grid=(n,),
        in_specs=[qkv(fwd), qkv(fwd), qkv(fwd), gsp(fwd), rsp(fwd),
                  qkv(bwd), qkv(bwd), qkv(bwd), gsp(bwd), rsp(bwd), sspec],
        out_specs=[qkv(fwd), qkv(bwd), sspec],
        out_shape=[jax.ShapeDtypeStruct((T, HD), F32), jax.ShapeDtypeStruct((T, HD), F32),
                   jax.ShapeDtypeStruct((2, DN_HEADS, DN_DK, DN_DK), F32)],
        scratch_shapes=[pltpu.VMEM((2, DN_HEADS, DN_DK, DN_DK), F32)],
        compiler_params=_cparams("arbitrary"),
        name="gdn_chunk",
    )(q, k, v, g, g_row, q, k, v, g, g_row, s0)


def _gdn_out_kernel(of_ref, ob_ref, z_ref, ng_ref, w_ref, h_ref, m_ref, o_ref):
    ys = []
    for h in range(DN_HEADS):
        hs = slice(h * DN_DK, (h + 1) * DN_DK)
        o = of_ref[:, hs] + ob_ref[:, hs]
        y = o * lax.rsqrt(jnp.mean(o * o, axis=-1, keepdims=True) + RMS_EPS) * ng_ref[...]
        ys.append((y * _silu(z_ref[:, hs])).astype(BF16))
    out = jnp.dot(jnp.concatenate(ys, axis=1), w_ref[...], preferred_element_type=F32)
    o_ref[...] = h_ref[...] + m_ref[...] * out


def gdn_out(o_f, o_b, p, norm_g, w_out_bf, h, gate):
    T, D = h.shape
    tm = min(T, 512)
    rows = lambda n: pl.BlockSpec((tm, n), lambda i: (i, 0))
    return pl.pallas_call(
        _gdn_out_kernel,
        grid=(T // tm,),
        in_specs=[rows(D), rows(D),
                  pl.BlockSpec((tm, D), lambda i: (i, DN_CONV_CH // D)),
                  pl.BlockSpec((1, DN_DK), lambda i: (0, 0)),
                  pl.BlockSpec((D, D), lambda i: (0, 0)),
                  rows(D),
                  pl.BlockSpec((1, D), lambda i: (0, 0))],
        out_specs=rows(D),
        out_shape=jax.ShapeDtypeStruct((T, D), F32),
        compiler_params=_cparams("parallel"),
        name="gdn_out",
    )(o_f, o_b, p, norm_g.reshape(1, -1), w_out_bf, h, gate.reshape(1, D))


def kernel(x, c, ctx, c_ctx, ada_w, ada_b, norm1_g, norm2_g, e_w_in, e_conv_w, s5_lam_re, s5_lam_im,
           s5_log_dt, s5_b_re, s5_b_im, s5_c_re, s5_c_im, s5_d, s5_glu_w, s5_glu_b, e_w_out, o_w_in,
           dn_conv_w, dn_a_log, dn_dt_bias, dn_norm_g, o_w_out, peer_wq, peer_subkeys, peer_u, peer_v,
           final_g):
    D = D_MODEL
    h_lat = x[0]
    h_ctx = ctx[0]
    ctx_len = h_ctx.shape[0]
    cv = jnp.zeros((SUBLANES, D), F32).at[0].set(c[0]).at[1].set(c_ctx)
    mods = adaln(cv, ada_w, ada_b)
    depth = ada_w.shape[0]
    for l in range(depth):
        last = l == depth - 1
        i = l // 2
        m_lat = mods[l, 0].reshape(N_MOD, D)
        m_ctx = mods[l, 1].reshape(N_MOD, D)
        if l % 2 == 0:
            w_in = e_w_in[i].astype(BF16)
            p_lat, _ = nmm(h_lat, norm1_g[l], m_lat[0], m_lat[1], w_in)
            p_ctx, _ = nmm(h_ctx, norm1_g[l], m_ctx[0], m_ctx[1], w_in)
            u_lat = _to_segments(p_lat[:, 3 * CONV_CH:])
            u_ctx = _to_segments(p_ctx[:, 3 * CONV_CH:])
            y_lat, y_ctx = [], []
            zero = (jnp.zeros((SSM_STATE,), F32), jnp.zeros((SSM_STATE,), F32))
            for di in range(2):
                bd, cd, ab_re, ab_im = s5_params(s5_lam_re[i, di], s5_lam_im[i, di], s5_log_dt[i, di],
                                                 s5_b_re[i, di], s5_b_im[i, di], s5_c_re[i, di],
                                                 s5_c_im[i, di])
                yc, fin = s5_direction(u_ctx, bd, cd, ab_re, ab_im, zero, di == 1)
                yl, _ = s5_direction(u_lat, bd, cd, ab_re, ab_im, fin, di == 1)
                y_ctx.append(_from_segments(yc))
                y_lat.append(_from_segments(yl))
            glu_w = s5_glu_w[i].astype(BF16)
            w_out = e_w_out[i].astype(BF16)
            h_lat = even_post(p_lat, y_lat[0], y_lat[1], h_lat, e_conv_w[i], s5_d[i], glu_w, s5_glu_b[i],
                              w_out, m_lat[2], GRID_W)
            if not last:
                h_ctx = even_post(p_ctx, y_ctx[0], y_ctx[1], h_ctx, e_conv_w[i], s5_d[i], glu_w,
                                  s5_glu_b[i], w_out, m_ctx[2], ctx_len)
        else:
            w_in = jnp.pad(o_w_in[i], ((0, 0), (0, DN_IN_PAD - o_w_in.shape[2]))).astype(BF16)
            p_lat, _ = nmm(h_lat, norm1_g[l], m_lat[0], m_lat[1], w_in)
            p_ctx, _ = nmm(h_ctx, norm1_g[l], m_ctx[0], m_ctx[1], w_in)
            qc, kc, vc, gc, rc = gdn_prep(p_ctx, dn_conv_w[i], dn_a_log[i], dn_dt_bias[i], ctx_len)
            ql, kl, vl, gl, rl = gdn_prep(p_lat, dn_conv_w[i], dn_a_log[i], dn_dt_bias[i], GRID_W)
            zero = jnp.zeros((2, DN_HEADS, DN_DK, DN_DK), F32)
            oc_f, oc_b, s_ctx = gdn_chunk(qc, kc, vc, gc, rc, zero)
            ol_f, ol_b, _ = gdn_chunk(ql, kl, vl, gl, rl, s_ctx)
            w_out = o_w_out[i].astype(BF16)
            h_lat = gdn_out(ol_f, ol_b, p_lat, dn_norm_g[i], w_out, h_lat, m_lat[2])
            if not last:
                h_ctx = gdn_out(oc_f, oc_b, p_ctx, dn_norm_g[i], w_out, h_ctx, m_ctx[2])
        wq = peer_wq[l].astype(BF16)
        u_bf = peer_u[l].astype(BF16)
        vt_bf = peer_v[l].T.astype(BF16)
        h_lat = peer_block(h_lat, norm2_g[l], m_lat[3], m_lat[4], m_lat[5], wq, peer_subkeys[l], u_bf,
                           vt_bf, final_g if last else None)
        if not last:
            h_ctx = peer_block(h_ctx, norm2_g[l], m_ctx[3], m_ctx[4], m_ctx[5], wq, peer_subkeys[l], u_bf,
                               vt_bf)
    return h_lat[None]
```
